```python
import math
import jax
import jax.numpy as jnp
from jax import lax
import numpy as np

D_MODEL = 2048
BATCH = 8
SEQ = 2048
DEPTH = 1
DEC_BATCH = 8
DEC_SEQ = 64
PAST_LEN = 2048

CHUNK = 64
Q_BLOCK = 128
ATT_HEADS = 8
ATT_HD = 64
ATT_WIDTH = ATT_HEADS * 2 * ATT_HD
ROPE_DIM = ATT_HD // 4
ROPE_THETA = 500000.0
SSM_WIDTH = D_MODEL - ATT_WIDTH
SSM_P = 16
SSM_GROUPS = SSM_WIDTH // SSM_P
SSM_N = 64
DT_MIN = 0.001
DT_MAX = 0.1
IN_WIDTH = 3 * ATT_WIDTH + SSM_WIDTH
PEER_HEADS = 8
PEER_KEYS = 128
PEER_EXPERTS = PEER_KEYS * PEER_KEYS
PEER_QDIM = 256
PEER_HALF = PEER_QDIM // 2
PEER_TOPK = 16
PEER_BLOCK = 128
EPS = 1e-6
NEG_INF = -1e30

kernel_name = 'hymba_diffattn_s5_peer_stream_step'


def rms_norm(x, g):
    xf = x.astype(jnp.float32)
    y = xf * lax.rsqrt(jnp.mean(xf * xf, axis=-1, keepdims=True) + EPS)
    return (y * g.astype(jnp.float32)).astype(x.dtype)


def partial_rope(t, pos):
    half = ROPE_DIM // 2
    inv_freq = ROPE_THETA ** (-jnp.arange(half, dtype=jnp.float32) * (2.0 / ROPE_DIM))
    ang = pos.astype(jnp.float32)[:, None] * inv_freq[None, :]
    cos = jnp.cos(ang)[None, :, None, None, :]
    sin = jnp.sin(ang)[None, :, None, None, :]
    tf = t.astype(jnp.float32)
    t1 = tf[..., :half]
    t2 = tf[..., half:ROPE_DIM]
    out = jnp.concatenate([t1 * cos - t2 * sin, t2 * cos + t1 * sin, tf[..., ROPE_DIM:]], axis=-1)
    return out.astype(t.dtype)


def diff_lambda(lq1, lk1, lq2, lk2, lam_init):
    f32 = jnp.float32
    return (jnp.exp(jnp.sum(lq1.astype(f32) * lk1.astype(f32)))
            - jnp.exp(jnp.sum(lq2.astype(f32) * lk2.astype(f32))) + lam_init)


def diff_attn_block(q, k, v, q_chunk, k_chunk, lam):
    s = jnp.einsum('bqhcd,bkhcd->bhcqk', q, k).astype(jnp.float32) * (ATT_HD ** -0.5)
    visible = k_chunk[None, :] <= q_chunk[:, None]
    s = jnp.where(visible, s, NEG_INF)
    p = jax.nn.softmax(s, axis=-1)
    w = p[:, :, 0] - lam * p[:, :, 1]
    return jnp.einsum('bhqk,bkhe->bqhe', w.astype(v.dtype), v)


def ssm_mixer(u, h0_re, h0_im, a_re, a_im, log_dt, b_re, b_im, c_re, c_im, d_skip, w_glu, b_glu):
    f32 = jnp.float32
    B, L, _ = u.shape
    uf = u.astype(f32).reshape(B, L, SSM_GROUPS, SSM_P)
    dt = jnp.exp(log_dt.astype(f32))[:, None]
    ar = a_re.astype(f32)
    ai = a_im.astype(f32)
    mag = jnp.exp(dt * ar)
    abar_r = mag * jnp.cos(dt * ai)
    abar_i = mag * jnp.sin(dt * ai)
    den = ar * ar + ai * ai
    nr = abar_r - 1.0
    ni = abar_i
    coef_r = (nr * ar + ni * ai) / den
    coef_i = (ni * ar - nr * ai) / den
    br = b_re.astype(f32)
    bi = b_im.astype(f32)
    bbar_r = coef_r[..., None] * br - coef_i[..., None] * bi
    bbar_i = coef_r[..., None] * bi + coef_i[..., None] * br
    bu_r = jnp.einsum('blgp,gnp->blgn', uf, bbar_r)
    bu_i = jnp.einsum('blgp,gnp->blgn', uf, bbar_i)
    a_r = jnp.broadcast_to(abar_r, (1, L, SSM_GROUPS, SSM_N))
    a_i = jnp.broadcast_to(abar_i, (1, L, SSM_GROUPS, SSM_N))

    def combine(e1, e2):
        a1r, a1i, b1r, b1i = e1
        a2r, a2i, b2r, b2i = e2
        return (a2r * a1r - a2i * a1i, a2r * a1i + a2i * a1r,
                a2r * b1r - a2i * b1i + b2r, a2r * b1i + a2i * b1r + b2i)

    pr, pi, hr, hi = lax.associative_scan(combine, (a_r, a_i, bu_r, bu_i), axis=1)
    if h0_re is not None:
        h0r = h0_re.astype(f32)[:, None]
        h0i = h0_im.astype(f32)[:, None]
        hr = hr + pr * h0r - pi * h0i
        hi = hi + pr * h0i + pi * h0r
    y = (jnp.einsum('blgn,gpn->blgp', hr, c_re.astype(f32))
         - jnp.einsum('blgn,gpn->blgp', hi, c_im.astype(f32))
         + d_skip.astype(f32).reshape(SSM_GROUPS, SSM_P) * uf)
    y = jax.nn.gelu(y.reshape(B, L, SSM_WIDTH))
    y = y * jax.nn.sigmoid(y @ w_glu.astype(f32) + b_glu.astype(f32))
    return y.astype(u.dtype), hr[:, -1].astype(u.dtype), hi[:, -1].astype(u.dtype)


def peer_ffn(x, w_peer_q, peer_keys, peer_u, peer_v):
    B, L, D = x.shape
    T = B * L
    xt = x.reshape(T, D)
    q = (xt @ w_peer_q).reshape(T, PEER_HEADS, 2, PEER_HALF)
    s = jnp.einsum('thcd,hcnd->thcn', q, peer_keys).astype(jnp.float32)
    sv, si = lax.top_k(s, PEER_TOPK)
    cand = (sv[:, :, 0, :, None] + sv[:, :, 1, None, :]).reshape(T, PEER_HEADS, PEER_TOPK * PEER_TOPK)
    cidx = (si[:, :, 0, :, None] * PEER_KEYS + si[:, :, 1, None, :]).reshape(T, PEER_HEADS, PEER_TOPK * PEER_TOPK)
    fv, fp = lax.top_k(cand, PEER_TOPK)
    eidx = jnp.take_along_axis(cidx, fp, axis=-1)
    g = jax.nn.softmax(fv, axis=-1).astype(x.dtype)
    nblk = -(-T // PEER_BLOCK)
    pad = nblk * PEER_BLOCK - T
    xp = jnp.pad(xt, ((0, pad), (0, 0))).reshape(nblk, PEER_BLOCK, D)
    ep = jnp.pad(eidx, ((0, pad), (0, 0), (0, 0))).reshape(nblk, PEER_BLOCK, PEER_HEADS, PEER_TOPK)
    gp = jnp.pad(g, ((0, pad), (0, 0), (0, 0))).reshape(nblk, PEER_BLOCK, PEER_HEADS, PEER_TOPK)

    def expert_block(args):
        xb, eb, gb = args
        u = jnp.take(peer_u, eb, axis=0)
        act = jax.nn.gelu(jnp.einsum('td,thkd->thk', xb, u))
        v = jnp.take(peer_v, eb, axis=0)
        return jnp.einsum('thk,thkd->td', gb * act, v)

    out = lax.map(expert_block, (xp, ep, gp)).reshape(nblk * PEER_BLOCK, D)[:T]
    return out.reshape(B, L, D)


def hybrid_layer(x, pos, past_k, past_v, h0_re, h0_im, lam_init, params):
    (norm1_g, w_in, q_norm_g, k_norm_g, lambda_q1, lambda_k1, lambda_q2, lambda_k2, subln_g,
     ssm_a_re, ssm_a_im, ssm_log_dt, ssm_b_re, ssm_b_im, ssm_c_re, ssm_c_im, ssm_d, w_glu, b_glu,
     w_out, norm2_g, w_peer_q, peer_keys, peer_u, peer_v) = params
    B, L, _ = x.shape
    proj = rms_norm(x, norm1_g) @ w_in
    q = proj[..., :ATT_WIDTH].reshape(B, L, ATT_HEADS, 2, ATT_HD)
    k = proj[..., ATT_WIDTH:2 * ATT_WIDTH].reshape(B, L, ATT_HEADS, 2, ATT_HD)
    v = proj[..., 2 * ATT_WIDTH:3 * ATT_WIDTH].reshape(B, L, ATT_HEADS, 2 * ATT_HD)
    u = proj[..., 3 * ATT_WIDTH:]
    q = partial_rope(rms_norm(q, q_norm_g), pos)
    k = partial_rope(rms_norm(k, k_norm_g), pos)
    lam = diff_lambda(lambda_q1, lambda_k1, lambda_q2, lambda_k2, lam_init)
    q_chunk = pos // CHUNK
    if past_k is None:
        nb = L // Q_BLOCK
        qb = jnp.moveaxis(q.reshape(B, nb, Q_BLOCK, ATT_HEADS, 2, ATT_HD), 1, 0)
        qcb = q_chunk.reshape(nb, Q_BLOCK)
        o = lax.map(lambda a: diff_attn_block(a[0], k, v, a[1], q_chunk, lam), (qb, qcb))
        o = jnp.moveaxis(o, 0, 1).reshape(B, L, ATT_HEADS, 2 * ATT_HD)
    else:
        P = past_k.shape[1]
        k_all = jnp.concatenate([past_k, k], axis=1)
        v_all = jnp.concatenate([past_v, v], axis=1)
        k_chunk = jnp.arange(P + L, dtype=jnp.int32) // CHUNK
        o = diff_attn_block(q, k_all, v_all, q_chunk, k_chunk, lam)
    o = rms_norm(o, subln_g) * (1.0 - lam_init)
    s, h_re, h_im = ssm_mixer(u, h0_re, h0_im, ssm_a_re, ssm_a_im, ssm_log_dt, ssm_b_re, ssm_b_im,
                              ssm_c_re, ssm_c_im, ssm_d, w_glu, b_glu)
    mix = jnp.concatenate([o.reshape(B, L, ATT_WIDTH), s], axis=-1)
    x = x + mix @ w_out
    x = x + peer_ffn(rms_norm(x, norm2_g), w_peer_q, peer_keys, peer_u, peer_v)
    return x, k, v, h_re, h_im


def setup_inputs(seed: int = 0) -> dict:
    key = jax.random.key(seed)
    ks = jax.random.split(key, 32)
    f32 = jnp.float32

    def nrm(k, shape, scale):
        return scale * jax.random.normal(k, shape, f32)

    a_im_base = jnp.pi * jnp.arange(SSM_N, dtype=f32)
    return {
        'x_prompt': nrm(ks[0], (BATCH, SEQ, D_MODEL), 1.0),
        'x_sample': nrm(ks[1], (DEC_BATCH, DEC_SEQ, D_MODEL), 1.0),
        'cache_k': nrm(ks[2], (DEPTH, DEC_BATCH, PAST_LEN, ATT_HEADS, 2, ATT_HD), 1.0),
        'cache_v': nrm(ks[3], (DEPTH, DEC_BATCH, PAST_LEN, ATT_HEADS, 2 * ATT_HD), 1.0),
        'state_ssm_re': nrm(ks[4], (DEPTH, DEC_BATCH, SSM_GROUPS, SSM_N), 0.5),
        'state_ssm_im': nrm(ks[5], (DEPTH, DEC_BATCH, SSM_GROUPS, SSM_N), 0.5),
        'norm1_g': 1.0 + nrm(ks[6], (DEPTH, D_MODEL), 0.02),
        'w_in': nrm(ks[7], (DEPTH, D_MODEL, IN_WIDTH), D_MODEL ** -0.5),
        'q_norm_g': 1.0 + nrm(ks[8], (DEPTH, ATT_HD), 0.02),
        'k_norm_g': 1.0 + nrm(ks[9], (DEPTH, ATT_HD), 0.02),
        'lambda_q1': nrm(ks[10], (DEPTH, ATT_HD), 0.1),
        'lambda_k1': nrm(ks[11], (DEPTH, ATT_HD), 0.1),
        'lambda_q2': nrm(ks[12], (DEPTH, ATT_HD), 0.1),
        'lambda_k2': nrm(ks[13], (DEPTH, ATT_HD), 0.1),
        'subln_g': 1.0 + nrm(ks[14], (DEPTH, 2 * ATT_HD), 0.02),
        'ssm_a_re': -0.5 + nrm(ks[15], (DEPTH, SSM_GROUPS, SSM_N), 0.01),
        'ssm_a_im': a_im_base + nrm(ks[16], (DEPTH, SSM_GROUPS, SSM_N), 0.01),
        'ssm_log_dt': jax.random.uniform(ks[17], (DEPTH, SSM_GROUPS), f32,
                                         minval=math.log(DT_MIN), maxval=math.log(DT_MAX)),
        'ssm_b_re': nrm(ks[18], (DEPTH, SSM_GROUPS, SSM_N, SSM_P), (2 * SSM_P) ** -0.5),
        'ssm_b_im': nrm(ks[19], (DEPTH, SSM_GROUPS, SSM_N, SSM_P), (2 * SSM_P) ** -0.5),
        'ssm_c_re': nrm(ks[20], (DEPTH, SSM_GROUPS, SSM_P, SSM_N), (2 * SSM_N) ** -0.5),
        'ssm_c_im': nrm(ks[21], (DEPTH, SSM_GROUPS, SSM_P, SSM_N), (2 * SSM_N) ** -0.5),
        'ssm_d': nrm(ks[22], (DEPTH, SSM_WIDTH), 1.0),
        'w_glu': nrm(ks[23], (DEPTH, SSM_WIDTH, SSM_WIDTH), SSM_WIDTH ** -0.5),
        'b_glu': nrm(ks[24], (DEPTH, SSM_WIDTH), 0.01),
        'w_out': nrm(ks[25], (DEPTH, D_MODEL, D_MODEL), D_MODEL ** -0.5),
        'norm2_g': 1.0 + nrm(ks[26], (DEPTH, D_MODEL), 0.02),
        'w_peer_q': nrm(ks[27], (DEPTH, D_MODEL, PEER_HEADS * PEER_QDIM), D_MODEL ** -0.5),
        'peer_keys': nrm(ks[28], (DEPTH, PEER_HEADS, 2, PEER_KEYS, PEER_HALF), PEER_HALF ** -0.5),
        'peer_u': nrm(ks[29], (DEPTH, PEER_EXPERTS, D_MODEL), D_MODEL ** -0.5),
        'peer_v': nrm(ks[30], (DEPTH, PEER_EXPERTS, D_MODEL), PEER_HEADS ** -0.5),
    }


def reference(x_prompt, x_sample, cache_k, cache_v, state_ssm_re, state_ssm_im,
              norm1_g, w_in, q_norm_g, k_norm_g, lambda_q1, lambda_k1, lambda_q2, lambda_k2, subln_g,
              ssm_a_re, ssm_a_im, ssm_log_dt, ssm_b_re, ssm_b_im, ssm_c_re, ssm_c_im, ssm_d,
              w_glu, b_glu, w_out, norm2_g, w_peer_q, peer_keys, peer_u, peer_v):
    Lp = x_prompt.shape[1]
    Ls = x_sample.shape[1]
    past = cache_k.shape[2]
    pos_p = jnp.arange(Lp, dtype=jnp.int32)
    pos_s = past + jnp.arange(Ls, dtype=jnp.int32)
    yp = x_prompt
    ys = x_sample
    kp_l, vp_l, hrp_l, hip_l = [], [], [], []
    ks_l, vs_l, hrs_l, his_l = [], [], [], []
    for layer in range(DEPTH):
        lam_init = 0.8 - 0.6 * math.exp(-0.3 * layer)
        params = (norm1_g[layer], w_in[layer], q_norm_g[layer], k_norm_g[layer],
                  lambda_q1[layer], lambda_k1[layer], lambda_q2[layer], lambda_k2[layer], subln_g[layer],
                  ssm_a_re[layer], ssm_a_im[layer], ssm_log_dt[layer], ssm_b_re[layer], ssm_b_im[layer],
                  ssm_c_re[layer], ssm_c_im[layer], ssm_d[layer], w_glu[layer], b_glu[layer],
                  w_out[layer], norm2_g[layer], w_peer_q[layer], peer_keys[layer],
                  peer_u[layer], peer_v[layer])
        yp, kp, vp, hrp, hip = hybrid_layer(yp, pos_p, None, None, None, None, lam_init, params)
        ys, kk, vv, hrs, his = hybrid_layer(ys, pos_s, cache_k[layer], cache_v[layer],
                                            state_ssm_re[layer], state_ssm_im[layer], lam_init, params)
        kp_l.append(kp); vp_l.append(vp); hrp_l.append(hrp); hip_l.append(hip)
        ks_l.append(kk); vs_l.append(vv); hrs_l.append(hrs); his_l.append(his)
    return (yp, ys, jnp.stack(kp_l), jnp.stack(vp_l), jnp.stack(hrp_l), jnp.stack(hip_l),
            jnp.stack(ks_l), jnp.stack(vs_l), jnp.stack(hrs_l), jnp.stack(his_l))
```

```python
import functools
import math

import jax
import jax.numpy as jnp
from jax import lax
from jax.experimental import pallas as pl
from jax.experimental.pallas import tpu as pltpu

F32 = jnp.float32
BF16 = jnp.bfloat16

CHUNK = 64
ATT_HEADS = 8
ATT_HD = 64
HEAD_W = 2 * ATT_HD
ATT_WIDTH = ATT_HEADS * HEAD_W
ROPE_DIM = ATT_HD // 4
ROPE_THETA = 500000.0
SSM_P = 16
SSM_N = 64
PEER_HEADS = 8
PEER_KEYS = 128
PEER_HALF = 128
PEER_TOPK = 16
EPS = 1e-6
NEG_INF = -1e30

LANES = 128
SUBLANES = 8
VMEM_LIMIT_BYTES = 56 * 1024 * 1024

INPROJ_ROWS = 256
ATTN_Q_ROWS = 256
SSM_STEPS = 32
SSM_GROUP_BLOCK = 16
SCAN_COLS = 512
OUT_ROWS = 256
TOPK_TOKENS = 128
EXPERT_TOKENS = 128
EXPERT_SLOTS = 4


def _cparams(sem):
    return pltpu.CompilerParams(dimension_semantics=sem, vmem_limit_bytes=VMEM_LIMIT_BYTES)


def _const_spec(shape):
    nd = len(shape)
    return pl.BlockSpec(shape, lambda *_: (0,) * nd, pipeline_mode=pl.Buffered(1))


def _split_bf16(a):
    hi = a.astype(BF16)
    lo = (a - hi.astype(F32)).astype(BF16)
    return hi, lo


def _dot(a, b):
    return jnp.dot(a, b, preferred_element_type=F32)


def _dot3(a_hi, a_lo, w_hi, w_lo):
    return _dot(a_hi, w_hi) + _dot(a_lo, w_hi) + _dot(a_hi, w_lo)


def _inproj_kernel(x_ref, g1_ref, w_ref, bd_ref, qg_ref, kg_ref, cos_ref, sa_ref, sb_ref,
                   qb_ref, kf_ref, kb_ref, vf_ref, vb_ref, u_ref):
    x = x_ref[...]
    ms = jnp.mean(x * x, axis=-1, keepdims=True)
    xn = (x * lax.rsqrt(ms + EPS) * g1_ref[...]).astype(BF16)
    proj = _dot(xn, w_ref[...])
    bd = bd_ref[...]
    cos, sa, sb = cos_ref[...], sa_ref[...], sb_ref[...]

    def head_norm_rope(t, g):
        hi, lo = _split_bf16(t * t)
        ssq = _dot(hi, bd) + _dot(lo, bd)
        n = t * lax.rsqrt(ssq * (1.0 / ATT_HD) + EPS) * g
        return n * cos + pltpu.roll(n, LANES - ROPE_DIM // 2, 1) * sa + pltpu.roll(n, ROPE_DIM // 2, 1) * sb

    for h in range(ATT_HEADS):
        c0 = h * HEAD_W
        q = head_norm_rope(proj[:, c0:c0 + HEAD_W], qg_ref[...])
        qb_ref[:, c0:c0 + HEAD_W] = (q * (ATT_HD ** -0.5)).astype(BF16)
        k = head_norm_rope(proj[:, ATT_WIDTH + c0:ATT_WIDTH + c0 + HEAD_W], kg_ref[...])
        kf_ref[:, c0:c0 + HEAD_W] = k
        kb_ref[:, c0:c0 + HEAD_W] = k.astype(BF16)
    v = proj[:, 2 * ATT_WIDTH:3 * ATT_WIDTH]
    vf_ref[...] = v
    vb_ref[...] = v.astype(BF16)
    u_ref[...] = proj[:, 3 * ATT_WIDTH:]


def _inproj(x2d, g1, w_bf, bd, qg, kg, cos, sa, sb, *, tm, pos_blocks):
    T, D = x2d.shape
    n_in = w_bf.shape[1]
    ssm_w = n_in - 3 * ATT_WIDTH
    row = lambda w: pl.BlockSpec((tm, w), lambda i: (i, 0))
    pos = pl.BlockSpec((tm, LANES), lambda i: (i % pos_blocks, 0))
    outs = [jax.ShapeDtypeStruct((T, ATT_WIDTH), BF16), jax.ShapeDtypeStruct((T, ATT_WIDTH), F32),
            jax.ShapeDtypeStruct((T, ATT_WIDTH), BF16), jax.ShapeDtypeStruct((T, ATT_WIDTH), F32),
            jax.ShapeDtypeStruct((T, ATT_WIDTH), BF16), jax.ShapeDtypeStruct((T, ssm_w), F32)]
    return pl.pallas_call(
        _inproj_kernel,
        grid=(T // tm,),
        in_specs=[row(D), _const_spec((1, D)), _const_spec((D, n_in)), _const_spec((LANES, LANES)),
                  _const_spec((1, LANES)), _const_spec((1, LANES)), pos, pos, pos],
        out_specs=[row(ATT_WIDTH)] * 5 + [row(ssm_w)],
        out_shape=outs,
        compiler_params=_cparams(("parallel",)),
        name="inproj",
    )(x2d, g1, w_bf, bd, qg, kg, cos, sa, sb)


def _diff_lambda(lam_ref, lam_init):
    lv = lam_ref[...]
    a = jnp.sum(lv[0:1] * lv[1:2], axis=-1, keepdims=True)
    b = jnp.sum(lv[2:3] * lv[3:4], axis=-1, keepdims=True)
    return jnp.exp(a) - jnp.exp(b) + lam_init


def _stack_maps(q):
    lane = lax.broadcasted_iota(jnp.int32, q.shape, 1)
    zero = jnp.zeros_like(q)
    return jnp.concatenate([jnp.where(lane < ATT_HD, q, zero), jnp.where(lane >= ATT_HD, q, zero)], axis=0)


def _attn_finish(acc, l, tq, lam, g, lam_init):
    o = acc[:tq] / l[:tq] - lam * (acc[tq:] / l[tq:])
    ms = jnp.mean(o * o, axis=-1, keepdims=True)
    return (o * lax.rsqrt(ms + EPS) * g * (1.0 - lam_init)).astype(BF16)


def _qk(qq, kb):
    return lax.dot_general(qq, kb, (((1,), (1,)), ((), ())), preferred_element_type=F32)


def _attn_prompt_kernel(lam_ref, g_ref, q_ref, k_ref, v_ref, o_ref, *, tq, lam_init):
    i = pl.program_id(2)
    qq = _stack_maps(q_ref[...])

    def update(carry, s, vb):
        m, l, acc = carry
        m_new = jnp.maximum(m, jnp.max(s, axis=1, keepdims=True))
        alpha = jnp.exp(m - m_new)
        p = jnp.exp(s - m_new)
        l = alpha * l + jnp.sum(p, axis=1, keepdims=True)
        acc = alpha * acc + _dot(p.astype(BF16), vb)
        return m_new, l, acc

    def full_block(j, carry):
        rows = pl.ds(pl.multiple_of(j * tq, tq), tq)
        return update(carry, _qk(qq, k_ref[rows, :]), v_ref[rows, :])

    carry = (jnp.full((2 * tq, 1), NEG_INF, F32), jnp.zeros((2 * tq, 1), F32),
             jnp.zeros((2 * tq, HEAD_W), F32))
    carry = lax.fori_loop(0, i, full_block, carry)
    rows = pl.ds(pl.multiple_of(i * tq, tq), tq)
    s = _qk(qq, k_ref[rows, :])
    r = lax.broadcasted_iota(jnp.int32, s.shape, 0) % tq
    c = lax.broadcasted_iota(jnp.int32, s.shape, 1)
    s = jnp.where(c // CHUNK <= r // CHUNK, s, NEG_INF)
    _, l, acc = update(carry, s, v_ref[rows, :])
    o_ref[...] = _attn_finish(acc, l, tq, _diff_lambda(lam_ref, lam_init), g_ref[...], lam_init)


def _attn_prompt(lam_vecs, subln_g, qb, kb, vb, *, batch, seq, lam_init):
    tq = ATTN_Q_ROWS
    nq = seq // tq
    T = batch * seq
    qspec = pl.BlockSpec((tq, HEAD_W), lambda b, h, i: (b * nq + i, h))
    kvspec = pl.BlockSpec((seq, HEAD_W), lambda b, h, i: (b, h))
    return pl.pallas_call(
        functools.partial(_attn_prompt_kernel, tq=tq, lam_init=lam_init),
        grid=(batch, ATT_HEADS, nq),
        in_specs=[_const_spec((4, ATT_HD)), _const_spec((1, HEAD_W)), qspec, kvspec, kvspec],
        out_specs=qspec,
        out_shape=jax.ShapeDtypeStruct((T, ATT_WIDTH), BF16),
        compiler_params=_cparams(("parallel", "parallel", "arbitrary")),
        name="attn_prompt",
    )(lam_vecs, subln_g, qb, kb, vb)


def _attn_sample_kernel(lam_ref, g_ref, q_ref, kp_ref, vp_ref, kn_ref, vn_ref, o_ref, *, tq, lam_init):
    qq = _stack_maps(q_ref[...])
    s_past = _qk(qq, kp_ref[...].astype(BF16))
    s_new = _qk(qq, kn_ref[...])
    m = jnp.maximum(jnp.max(s_past, axis=1, keepdims=True), jnp.max(s_new, axis=1, keepdims=True))
    p_past = jnp.exp(s_past - m)
    p_new = jnp.exp(s_new - m)
    l = jnp.sum(p_past, axis=1, keepdims=True) + jnp.sum(p_new, axis=1, keepdims=True)
    acc = _dot(p_past.astype(BF16), vp_ref[...].astype(BF16)) + _dot(p_new.astype(BF16), vn_ref[...])
    o_ref[...] = _attn_finish(acc, l, tq, _diff_lambda(lam_ref, lam_init), g_ref[...], lam_init)


def _attn_sample(lam_vecs, subln_g, qb, k_past, v_past, kb, vb, *, batch, seq, past, lam_init):
    new = pl.BlockSpec((seq, HEAD_W), lambda b, h: (b, h))
    old = pl.BlockSpec((past, HEAD_W), lambda b, h: (b, h))
    return pl.pallas_call(
        functools.partial(_attn_sample_kernel, tq=seq, lam_init=lam_init),
        grid=(batch, ATT_HEADS),
        in_specs=[_const_spec((4, ATT_HD)), _const_spec((1, HEAD_W)), new, old, old, new, new],
        out_specs=new,
        out_shape=jax.ShapeDtypeStruct((batch * seq, ATT_WIDTH), BF16),
        compiler_params=_cparams(("parallel", "parallel")),
        name="attn_sample",
    )(lam_vecs, subln_g, qb, k_past, v_past, kb, vb)


def _ssm_kernel(u_ref, h0_ref, a_ref, bh_ref, bl_ref, crh_ref, crl_ref, cih_ref, cil_ref,
                d_ref, wg_ref, bg_ref, s_ref, hout_ref, bu_ref, h_ref, *, steps, streams):
    n_state = a_ref.shape[1]
    n_blocks = bh_ref.shape[0]
    in_w = bh_ref.shape[1]
    st_w = n_state // n_blocks

    @pl.when(pl.program_id(0) == 0)
    def _():
        h_ref[...] = h0_ref[...]

    u = u_ref[...]
    u_hi, u_lo = _split_bf16(u)
    for gb in range(n_blocks):
        cols = slice(gb * in_w, (gb + 1) * in_w)
        bu = _dot3(u_hi[:, cols], u_lo[:, cols], bh_ref[gb], bl_ref[gb])
        bu_ref[:, gb * st_w:(gb + 1) * st_w] = bu[:, :st_w]
        bu_ref[:, n_state + gb * st_w:n_state + (gb + 1) * st_w] = bu[:, st_w:]

    for c in range(n_state // SCAN_COLS):
        re = slice(c * SCAN_COLS, (c + 1) * SCAN_COLS)
        im = slice(n_state + c * SCAN_COLS, n_state + (c + 1) * SCAN_COLS)
        ar = jnp.broadcast_to(a_ref[0:1, re], (streams, SCAN_COLS))
        ai = jnp.broadcast_to(a_ref[1:2, re], (streams, SCAN_COLS))

        def step(t, carry):
            hr, hi = carry
            rows = pl.ds(pl.multiple_of(t * streams, streams), streams)
            nr = ar * hr - ai * hi + bu_ref[rows, re]
            ni = ar * hi + ai * hr + bu_ref[rows, im]
            bu_ref[rows, re] = nr
            bu_ref[rows, im] = ni
            return nr, ni

        hr, hi = lax.fori_loop(0, steps, step, (h_ref[0, :, re], h_ref[1, :, re]), unroll=4)
        h_ref[0, :, re] = hr
        h_ref[1, :, re] = hi
    hout_ref[...] = h_ref[...]

    ys = []
    for gb in range(n_blocks):
        hr_hi, hr_lo = _split_bf16(bu_ref[:, gb * st_w:(gb + 1) * st_w])
        hi_hi, hi_lo = _split_bf16(bu_ref[:, n_state + gb * st_w:n_state + (gb + 1) * st_w])
        ys.append(_dot3(hr_hi, hr_lo, crh_ref[gb], crl_ref[gb]) + _dot3(hi_hi, hi_lo, cih_ref[gb], cil_ref[gb]))
    y = jax.nn.gelu(jnp.concatenate(ys, axis=1) + d_ref[...] * u)
    z = _dot(y.astype(BF16), wg_ref[...]) + bg_ref[...]
    s_ref[...] = (y * jax.nn.sigmoid(z)).astype(BF16)


def _ssm(u_tm, h0, a, bh, bl, crh, crl, cih, cil, d, wg, bg, *, streams, steps):
    rows_total, width = u_tm.shape
    rows = steps * streams
    n_state = a.shape[1]
    row = pl.BlockSpec((rows, width), lambda i: (i, 0))
    consts = [h0, a, bh, bl, crh, crl, cih, cil, d, wg, bg]
    return pl.pallas_call(
        functools.partial(_ssm_kernel, steps=steps, streams=streams),
        grid=(rows_total // rows,),
        in_specs=[row] + [_const_spec(c.shape) for c in consts],
        out_specs=[row, _const_spec(h0.shape)],
        out_shape=[jax.ShapeDtypeStruct((rows_total, width), BF16), jax.ShapeDtypeStruct(h0.shape, F32)],
        scratch_shapes=[pltpu.VMEM((rows, 2 * n_state), F32), pltpu.VMEM(h0.shape, F32)],
        compiler_params=_cparams(("arbitrary",)),
        name="ssm",
    )(u_tm, *consts)


def _ssm_params(a_re, a_im, log_dt, b_re, b_im, c_re, c_im):
    G, N, P = b_re.shape
    gb_n = SSM_GROUP_BLOCK
    nb = G // gb_n
    dt = jnp.exp(log_dt.astype(F32))[:, None]
    ar, ai = a_re.astype(F32), a_im.astype(F32)
    mag = jnp.exp(dt * ar)
    abar_r = mag * jnp.cos(dt * ai)
    abar_i = mag * jnp.sin(dt * ai)
    den = ar * ar + ai * ai
    nr, ni = abar_r - 1.0, abar_i
    coef_r = (nr * ar + ni * ai) / den
    coef_i = (ni * ar - nr * ai) / den
    br, bi = b_re.astype(F32), b_im.astype(F32)
    bbar_r = coef_r[..., None] * br - coef_i[..., None] * bi
    bbar_i = coef_r[..., None] * bi + coef_i[..., None] * br
    eye = jnp.eye(gb_n, dtype=F32)

    def in_block(bb):
        w = bb.reshape(nb, gb_n, N, P).transpose(0, 1, 3, 2)
        return jnp.einsum('bgpn,gh->bgphn', w, eye).reshape(nb, gb_n * P, gb_n * N)

    def out_block(cc):
        w = cc.reshape(nb, gb_n, P, N).transpose(0, 1, 3, 2)
        return jnp.einsum('bgnp,gh->bgnhp', w, eye).reshape(nb, gb_n * N, gb_n * P)

    b_bd = jnp.concatenate([in_block(bbar_r), in_block(bbar_i)], axis=2)
    a = jnp.stack([abar_r.reshape(-1), abar_i.reshape(-1)])
    return a, _split_bf16(b_bd), _split_bf16(out_block(c_re.astype(F32))), _split_bf16(-out_block(c_im.astype(F32)))


def _outproj_kernel(x_ref, o_ref, s_ref, wo_ref, g2_ref, wq_ref, keys_ref, x1_ref, xn_ref, sc_ref):
    aw = o_ref.shape[1]
    x1 = x_ref[...] + _dot(o_ref[...], wo_ref[:aw, :]) + _dot(s_ref[...], wo_ref[aw:, :])
    x1_ref[...] = x1
    ms = jnp.mean(x1 * x1, axis=-1, keepdims=True)
    xn = x1 * lax.rsqrt(ms + EPS) * g2_ref[...]
    xn_ref[...] = xn
    q = _dot(xn.astype(BF16), wq_ref[...])
    for j in range(keys_ref.shape[0]):
        qj = q[:, j * PEER_HALF:(j + 1) * PEER_HALF].astype(BF16)
        sc_ref[j] = _qk(keys_ref[j], qj)


def _outproj(x2d, o_b, s_b, wo_bf, g2, wq_bf, keys_bf, *, tm):
    T, D = x2d.shape
    nset = keys_bf.shape[0]
    row = lambda w: pl.BlockSpec((tm, w), lambda i: (i, 0))
    return pl.pallas_call(
        _outproj_kernel,
        grid=(T // tm,),
        in_specs=[row(D), row(o_b.shape[1]), row(s_b.shape[1]), _const_spec(wo_bf.shape), _const_spec((1, D)),
                  _const_spec(wq_bf.shape), _const_spec(keys_bf.shape)],
        out_specs=[row(D), row(D), pl.BlockSpec((nset, PEER_KEYS, tm), lambda i: (0, 0, i))],
        out_shape=[jax.ShapeDtypeStruct((T, D), F32), jax.ShapeDtypeStruct((T, D), F32),
                   jax.ShapeDtypeStruct((nset, PEER_KEYS, T), F32)],
        compiler_params=_cparams(("parallel",)),
        name="outproj",
    )(x2d, o_b, s_b, wo_bf, g2, wq_bf, keys_bf)


def _extract_topk(vals, payload, k):
    n, tt = vals.shape
    pos = lax.broadcasted_iota(jnp.int32, (n, tt), 0).astype(F32)
    slot = lax.broadcasted_iota(jnp.int32, (k, tt), 0)
    out_v = jnp.zeros((k, tt), F32)
    out_p = jnp.zeros((k, tt), F32)
    for r in range(k):
        m = jnp.max(vals, axis=0, keepdims=True)
        first = jnp.min(jnp.where(vals == m, pos, float(n)), axis=0, keepdims=True)
        hit = pos == first
        picked = jnp.max(jnp.where(hit, payload, -1.0), axis=0, keepdims=True)
        out_v = jnp.where(slot == r, m, out_v)
        out_p = jnp.where(slot == r, picked, out_p)
        vals = jnp.where(hit, -jnp.inf, vals)
    return out_v, out_p


def _topk_kernel(sc_ref, idx_ref, gate_ref):
    k = PEER_TOPK
    tt = sc_ref.shape[2]
    key_id = lax.broadcasted_iota(jnp.int32, (PEER_KEYS, tt), 0).astype(F32)

    def head(h, _):
        v0, i0 = _extract_topk(sc_ref[2 * h], key_id, k)
        v1, i1 = _extract_topk(sc_ref[2 * h + 1], key_id, k)
        cand = jnp.concatenate([v0[i:i + 1] + v1 for i in range(k)], axis=0)
        cidx = jnp.concatenate([i0[i:i + 1] * float(PEER_KEYS) + i1 for i in range(k)], axis=0)
        fv, fe = _extract_topk(cand, cidx, k)
        e = jnp.exp(fv - fv[0:1])
        rows = pl.ds(pl.multiple_of(h * k, k), k)
        gate_ref[rows, :] = e / jnp.sum(e, axis=0, keepdims=True)
        idx_ref[rows, :] = fe.astype(jnp.int32)
        return 0

    lax.fori_loop(0, PEER_HEADS, head, 0)


def _topk(scores):
    nset, nkeys, T = scores.shape
    tt = TOPK_TOKENS
    out = pl.BlockSpec((PEER_HEADS * PEER_TOPK, tt), lambda i: (0, i))
    return pl.pallas_call(
        _topk_kernel,
        grid=(T // tt,),
        in_specs=[pl.BlockSpec((nset, nkeys, tt), lambda i: (0, 0, i))],
        out_specs=[out, out],
        out_shape=[jax.ShapeDtypeStruct((PEER_HEADS * PEER_TOPK, T), jnp.int32),
                   jax.ShapeDtypeStruct((PEER_HEADS * PEER_TOPK, T), F32)],
        compiler_params=_cparams(("parallel",)),
        name="peer_topk",
    )(scores)


def _expert_kernel(idx_ref, xn_ref, x1_ref, gate_ref, u_hbm, v_hbm, y_ref, buf, sem, *, tokens, slots):
    n_sel = buf.shape[1]
    lane = lax.broadcasted_iota(jnp.int32, (n_sel, tokens), 1)

    def issue(table, t):
        slot = t % slots

        def one(k, _):
            e = idx_ref[t, k]
            pltpu.make_async_copy(table.at[pl.ds(e, 1), :], buf.at[slot, pl.ds(k, 1), :], sem.at[slot]).start()
            return 0

        lax.fori_loop(0, n_sel, one, 0, unroll=8)

    def wait(table, t):
        slot = t % slots
        pltpu.make_async_copy(table.at[pl.ds(0, n_sel), :], buf.at[slot], sem.at[slot]).wait()

    def pipelined(table, body, init):
        for t in range(slots - 1):
            issue(table, t)

        def step(t, carry):
            @pl.when(t + slots - 1 < tokens)
            def _():
                issue(table, t + slots - 1)

            wait(table, t)
            return body(t, buf[t % slots], carry)

        return lax.fori_loop(0, tokens, step, init)

    def act_body(t, u_rows, acts):
        a = jnp.sum(u_rows * xn_ref[pl.ds(t, 1), :], axis=1, keepdims=True)
        return jnp.where(lane == t, a, acts)

    acts = pipelined(u_hbm, act_body, jnp.zeros((n_sel, tokens), F32))
    coef = gate_ref[...] * jax.nn.gelu(acts)

    def out_body(t, v_rows, carry):
        c = jnp.sum(jnp.where(lane == t, coef, 0.0), axis=1, keepdims=True)
        y_ref[pl.ds(t, 1), :] = x1_ref[pl.ds(t, 1), :] + jnp.sum(v_rows * c, axis=0, keepdims=True)
        return carry

    pipelined(v_hbm, out_body, 0)


def _experts(idx_tok, xn, x1, gates, peer_u, peer_v):
    T, D = xn.shape
    n_sel = idx_tok.shape[1]
    tt = EXPERT_TOKENS
    row = pl.BlockSpec((tt, D), lambda i: (i, 0))
    return pl.pallas_call(
        functools.partial(_expert_kernel, tokens=tt, slots=EXPERT_SLOTS),
        grid=(T // tt,),
        in_specs=[pl.BlockSpec((tt, n_sel), lambda i: (i, 0), memory_space=pltpu.SMEM), row, row,
                  pl.BlockSpec((n_sel, tt), lambda i: (0, i)),
                  pl.BlockSpec(memory_space=pl.ANY), pl.BlockSpec(memory_space=pl.ANY)],
        out_specs=row,
        out_shape=jax.ShapeDtypeStruct((T, D), F32),
        scratch_shapes=[pltpu.VMEM((EXPERT_SLOTS, n_sel, D), F32), pltpu.SemaphoreType.DMA((EXPERT_SLOTS,))],
        compiler_params=_cparams(("arbitrary",)),
        name="peer_experts",
    )(idx_tok, xn, x1, gates, peer_u, peer_v)


def _rope_tables(pos):
    half = ROPE_DIM // 2
    inv_freq = ROPE_THETA ** (-jnp.arange(half, dtype=F32) * (2.0 / ROPE_DIM))
    ang = pos.astype(F32)[:, None] * inv_freq[None, :]
    cos, sin = jnp.cos(ang), jnp.sin(ang)
    n = pos.shape[0]
    pad = jnp.zeros((n, ATT_HD - ROPE_DIM), F32)
    zero = jnp.zeros((n, half), F32)
    c = jnp.concatenate([cos, cos, pad + 1.0], axis=1)
    sa = jnp.concatenate([-sin, zero, pad], axis=1)
    sb = jnp.concatenate([zero, sin, pad], axis=1)
    return tuple(jnp.tile(t, (1, LANES // ATT_HD)) for t in (c, sa, sb))


def _layer(x, pos, past_k, past_v, h0_re, h0_im, lam_init, p):
    B, L, D = x.shape
    T = B * L
    x2d = x.reshape(T, D)
    tm = min(INPROJ_ROWS, L)
    cos, sa, sb = _rope_tables(pos)
    qb, kf, kb, vf, vb, u = _inproj(x2d, p['g1'], p['w_in'], p['bd'], p['qg'], p['kg'], cos, sa, sb,
                                   tm=tm, pos_blocks=L // tm)
    if past_k is None:
        o = _attn_prompt(p['lam'], p['subln'], qb, kb, vb, batch=B, seq=L, lam_init=lam_init)
    else:
        past = past_k.shape[1]
        o = _attn_sample(p['lam'], p['subln'], qb, past_k.reshape(B * past, ATT_WIDTH),
                         past_v.reshape(B * past, ATT_WIDTH), kb, vb, batch=B, seq=L, past=past,
                         lam_init=lam_init)
    n_state = p['a'].shape[1]
    if h0_re is None:
        h0 = jnp.zeros((2, B, n_state), F32)
    else:
        h0 = jnp.stack([h0_re.reshape(B, n_state), h0_im.reshape(B, n_state)]).astype(F32)
    ssm_w = u.shape[1]
    u_tm = u.reshape(B, L, ssm_w).transpose(1, 0, 2).reshape(T, ssm_w)
    s_tm, h_fin = _ssm(u_tm, h0, p['a'], *p['b'], *p['cr'], *p['ci'], p['d'], p['wg'], p['bg'],
                       streams=B, steps=min(SSM_STEPS, L))
    s = s_tm.reshape(L, B, ssm_w).transpose(1, 0, 2).reshape(T, ssm_w)
    x1, xn, scores = _outproj(x2d, o, s, p['w_out'], p['g2'], p['wq'], p['keys'], tm=min(OUT_ROWS, T))
    idx_t, gates = _topk(scores)
    y = _experts(idx_t.T, xn, x1, gates, p['peer_u'], p['peer_v'])
    G = n_state // SSM_N
    return (y.reshape(B, L, D), kf.reshape(B, L, ATT_HEADS, 2, ATT_HD), vf.reshape(B, L, ATT_HEADS, HEAD_W),
            h_fin[0].reshape(B, G, SSM_N), h_fin[1].reshape(B, G, SSM_N))


def kernel(x_prompt, x_sample, cache_k, cache_v, state_ssm_re, state_ssm_im, norm1_g, w_in, q_norm_g, k_norm_g, lambda_q1, lambda_k1, lambda_q2, lambda_k2, subln_g, ssm_a_re, ssm_a_im, ssm_log_dt, ssm_b_re, ssm_b_im, ssm_c_re, ssm_c_im, ssm_d, w_glu, b_glu, w_out, norm2_g, w_peer_q, peer_keys, peer_u, peer_v):
    depth = w_in.shape[0]
    Lp, Ls, past = x_prompt.shape[1], x_sample.shape[1], cache_k.shape[2]
    pos_p = jnp.arange(Lp, dtype=jnp.int32)
    pos_s = past + jnp.arange(Ls, dtype=jnp.int32)
    lane = jnp.arange(LANES)
    bd = (lane[:, None] // ATT_HD == lane[None, :] // ATT_HD).astype(BF16)
    yp, ys = x_prompt, x_sample
    outs = [[] for _ in range(8)]
    for layer in range(depth):
        lam_init = 0.8 - 0.6 * math.exp(-0.3 * layer)
        a, b, cr, ci = _ssm_params(ssm_a_re[layer], ssm_a_im[layer], ssm_log_dt[layer], ssm_b_re[layer],
                                   ssm_b_im[layer], ssm_c_re[layer], ssm_c_im[layer])
        p = dict(
            g1=norm1_g[layer][None].astype(F32), w_in=w_in[layer].astype(BF16), bd=bd,
            qg=jnp.tile(q_norm_g[layer].astype(F32), LANES // ATT_HD)[None],
            kg=jnp.tile(k_norm_g[layer].astype(F32), LANES // ATT_HD)[None],
            lam=jnp.stack([lambda_q1[layer], lambda_k1[layer], lambda_q2[layer], lambda_k2[layer]]).astype(F32),
            subln=subln_g[layer][None].astype(F32),
            a=a, b=b, cr=cr, ci=ci, d=ssm_d[layer][None].astype(F32), wg=w_glu[layer].astype(BF16),
            bg=b_glu[layer][None].astype(F32), w_out=w_out[layer].astype(BF16),
            g2=norm2_g[layer][None].astype(F32), wq=w_peer_q[layer].astype(BF16),
            keys=peer_keys[layer].reshape(PEER_HEADS * 2, PEER_KEYS, PEER_HALF).astype(BF16),
            peer_u=peer_u[layer], peer_v=peer_v[layer])
        yp, kp, vp, hrp, hip = _layer(yp, pos_p, None, None, None, None, lam_init, p)
        ys, kk, vv, hrs, his = _layer(ys, pos_s, cache_k[layer], cache_v[layer], state_ssm_re[layer],
                                      state_ssm_im[layer], lam_init, p)
        for lst, val in zip(outs, (kp, vp, hrp, hip, kk, vv, hrs, his)):
            lst.append(val)
    return (yp, ys) + tuple(jnp.stack(l) for l in outs)
```

```python
import functools
import math

import jax
import jax.numpy as jnp
from jax import lax
from jax.experimental import pallas as pl
from jax.experimental.pallas import tpu as pltpu

F32 = jnp.float32
BF16 = jnp.bfloat16

CHUNK = 64
ATT_HEADS = 8
ATT_HD = 64
HEAD_W = 2 * ATT_HD
ATT_WIDTH = ATT_HEADS * HEAD_W
ROPE_DIM = ATT_HD // 4
ROPE_THETA = 500000.0
SSM_P = 16
SSM_N = 64
PEER_HEADS = 8
PEER_KEYS = 128
PEER_HALF = 128
PEER_TOPK = 16
EPS = 1e-6
NEG_INF = -1e30

LANES = 128
SUBLANES = 8
VMEM_LIMIT_BYTES = 56 * 1024 * 1024

INPROJ_ROWS = 256
ATTN_Q_ROWS = 256
SSM_STEPS = 32
SSM_GROUP_BLOCK = 16
SCAN_COLS = 512
OUT_ROWS = 256
TOPK_TOKENS = 128
EXPERT_TOKENS = 128
EXPERT_SLOTS = 4


def _cparams(sem):
    return pltpu.CompilerParams(dimension_semantics=sem, vmem_limit_bytes=VMEM_LIMIT_BYTES)


def _const_spec(shape):
    nd = len(shape)
    return pl.BlockSpec(shape, lambda *_: (0,) * nd, pipeline_mode=pl.Buffered(1))


def _split_bf16(a):
    hi = a.astype(BF16)
    lo = (a - hi.astype(F32)).astype(BF16)
    return hi, lo


def _dot(a, b):
    return jnp.dot(a, b, preferred_element_type=F32)


def _dot3(a_hi, a_lo, w_hi, w_lo):
    return _dot(a_hi, w_hi) + _dot(a_lo, w_hi) + _dot(a_hi, w_lo)


def _inproj_kernel(x_ref, g1_ref, w_ref, bd_ref, qg_ref, kg_ref, cos_ref, sa_ref, sb_ref,
                   qb_ref, kf_ref, kb_ref, vf_ref, vb_ref, u_ref):
    x = x_ref[...]
    ms = jnp.mean(x * x, axis=-1, keepdims=True)
    xn = (x * lax.rsqrt(ms + EPS) * g1_ref[...]).astype(BF16)
    proj = _dot(xn, w_ref[...])
    bd = bd_ref[...]
    cos, sa, sb = cos_ref[...], sa_ref[...], sb_ref[...]

    def head_norm_rope(t, g):
        hi, lo = _split_bf16(t * t)
        ssq = _dot(hi, bd) + _dot(lo, bd)
        n = t * lax.rsqrt(ssq * (1.0 / ATT_HD) + EPS) * g
        return n * cos + pltpu.roll(n, LANES - ROPE_DIM // 2, 1) * sa + pltpu.roll(n, ROPE_DIM // 2, 1) * sb

    for h in range(ATT_HEADS):
        c0 = h * HEAD_W
        q = head_norm_rope(proj[:, c0:c0 + HEAD_W], qg_ref[...])
        qb_ref[:, c0:c0 + HEAD_W] = (q * (ATT_HD ** -0.5)).astype(BF16)
        k = head_norm_rope(proj[:, ATT_WIDTH + c0:ATT_WIDTH + c0 + HEAD_W], kg_ref[...])
        kf_ref[:, c0:c0 + HEAD_W] = k
        kb_ref[:, c0:c0 + HEAD_W] = k.astype(BF16)
    v = proj[:, 2 * ATT_WIDTH:3 * ATT_WIDTH]
    vf_ref[...] = v
    vb_ref[...] = v.astype(BF16)
    u_ref[...] = proj[:, 3 * ATT_WIDTH:]


def _inproj(x2d, g1, w_bf, bd, qg, kg, cos, sa, sb, *, tm, pos_blocks):
    T, D = x2d.shape
    n_in = w_bf.shape[1]
    ssm_w = n_in - 3 * ATT_WIDTH
    row = lambda w: pl.BlockSpec((tm, w), lambda i: (i, 0))
    pos = pl.BlockSpec((tm, LANES), lambda i: (i % pos_blocks, 0))
    outs = [jax.ShapeDtypeStruct((T, ATT_WIDTH), BF16), jax.ShapeDtypeStruct((T, ATT_WIDTH), F32),
            jax.ShapeDtypeStruct((T, ATT_WIDTH), BF16), jax.ShapeDtypeStruct((T, ATT_WIDTH), F32),
            jax.ShapeDtypeStruct((T, ATT_WIDTH), BF16), jax.ShapeDtypeStruct((T, ssm_w), F32)]
    return pl.pallas_call(
        _inproj_kernel,
        grid=(T // tm,),
        in_specs=[row(D), _const_spec((1, D)), _const_spec((D, n_in)), _const_spec((LANES, LANES)),
                  _const_spec((1, LANES)), _const_spec((1, LANES)), pos, pos, pos],
        out_specs=[row(ATT_WIDTH)] * 5 + [row(ssm_w)],
        out_shape=outs,
        compiler_params=_cparams(("parallel",)),
        name="inproj",
    )(x2d, g1, w_bf, bd, qg, kg, cos, sa, sb)


def _diff_lambda(lam_ref, lam_init):
    lv = lam_ref[...]
    a = jnp.sum(lv[0:1] * lv[1:2], axis=-1, keepdims=True)
    b = jnp.sum(lv[2:3] * lv[3:4], axis=-1, keepdims=True)
    return jnp.exp(a) - jnp.exp(b) + lam_init


def _stack_maps(q):
    lane = lax.broadcasted_iota(jnp.int32, q.shape, 1)
    zero = jnp.zeros_like(q)
    return jnp.concatenate([jnp.where(lane < ATT_HD, q, zero), jnp.where(lane >= ATT_HD, q, zero)], axis=0)


def _attn_finish(acc, l, tq, lam, g, lam_init):
    o = acc[:tq] / l[:tq] - lam * (acc[tq:] / l[tq:])
    ms = jnp.mean(o * o, axis=-1, keepdims=True)
    return (o * lax.rsqrt(ms + EPS) * g * (1.0 - lam_init)).astype(BF16)


def _qk(qq, kb):
    return lax.dot_general(qq, kb, (((1,), (1,)), ((), ())), preferred_element_type=F32)


def _attn_prompt_kernel(lam_ref, g_ref, q_ref, k_ref, v_ref, o_ref, *, tq, lam_init):
    i = pl.program_id(2)
    qq = _stack_maps(q_ref[...])

    def update(carry, s, vb):
        m, l, acc = carry
        m_new = jnp.maximum(m, jnp.max(s, axis=1, keepdims=True))
        alpha = jnp.exp(m - m_new)
        p = jnp.exp(s - m_new)
        l = alpha * l + jnp.sum(p, axis=1, keepdims=True)
        acc = alpha * acc + _dot(p.astype(BF16), vb)
        return m_new, l, acc

    def full_block(j, carry):
        rows = pl.ds(pl.multiple_of(j * tq, tq), tq)
        return update(carry, _qk(qq, k_ref[rows, :]), v_ref[rows, :])

    carry = (jnp.full((2 * tq, 1), NEG_INF, F32), jnp.zeros((2 * tq, 1), F32),
             jnp.zeros((2 * tq, HEAD_W), F32))
    carry = lax.fori_loop(0, i, full_block, carry)
    rows = pl.ds(pl.multiple_of(i * tq, tq), tq)
    s = _qk(qq, k_ref[rows, :])
    r = lax.broadcasted_iota(jnp.int32, s.shape, 0) % tq
    c = lax.broadcasted_iota(jnp.int32, s.shape, 1)
    s = jnp.where(c // CHUNK <= r // CHUNK, s, NEG_INF)
    _, l, acc = update(carry, s, v_ref[rows, :])
    o_ref[...] = _attn_finish(acc, l, tq, _diff_lambda(lam_ref, lam_init), g_ref[...], lam_init)


def _attn_prompt(lam_vecs, subln_g, qb, kb, vb, *, batch, seq, lam_init):
    tq = ATTN_Q_ROWS
    nq = seq // tq
    T = batch * seq
    qspec = pl.BlockSpec((tq, HEAD_W), lambda b, h, i: (b * nq + i, h))
    kvspec = pl.BlockSpec((seq, HEAD_W), lambda b, h, i: (b, h))
    return pl.pallas_call(
        functools.partial(_attn_prompt_kernel, tq=tq, lam_init=lam_init),
        grid=(batch, ATT_HEADS, nq),
        in_specs=[_const_spec((4, ATT_HD)), _const_spec((1, HEAD_W)), qspec, kvspec, kvspec],
        out_specs=qspec,
        out_shape=jax.ShapeDtypeStruct((T, ATT_WIDTH), BF16),
        compiler_params=_cparams(("parallel", "parallel", "arbitrary")),
        name="attn_prompt",
    )(lam_vecs, subln_g, qb, kb, vb)


def _attn_sample_kernel(lam_ref, g_ref, q_ref, kp_ref, vp_ref, kn_ref, vn_ref, o_ref, *, tq, lam_init):
    qq = _stack_maps(q_ref[...])
    s_past = _qk(qq, kp_ref[...].astype(BF16))
    s_new = _qk(qq, kn_ref[...])
    m = jnp.maximum(jnp.max(s_past, axis=1, keepdims=True), jnp.max(s_new, axis=1, keepdims=True))
    p_past = jnp.exp(s_past - m)
    p_new = jnp.exp(s_new - m)
    l = jnp.sum(p_past, axis=1, keepdims=True) + jnp.sum(p_new, axis=1, keepdims=True)
    acc = _dot(p_past.astype(BF16), vp_ref[...].astype(BF16)) + _dot(p_new.astype(BF16), vn_ref[...])
    o_ref[...] = _attn_finish(acc, l, tq, _diff_lambda(lam_ref, lam_init), g_ref[...], lam_init)


def _attn_sample(lam_vecs, subln_g, qb, k_past, v_past, kb, vb, *, batch, seq, past, lam_init):
    new = pl.BlockSpec((seq, HEAD_W), lambda b, h: (b, h))
    old = pl.BlockSpec((past, HEAD_W), lambda b, h: (b, h))
    return pl.pallas_call(
        functools.partial(_attn_sample_kernel, tq=seq, lam_init=lam_init),
        grid=(batch, ATT_HEADS),
        in_specs=[_const_spec((4, ATT_HD)), _const_spec((1, HEAD_W)), new, old, old, new, new],
        out_specs=new,
        out_shape=jax.ShapeDtypeStruct((batch * seq, ATT_WIDTH), BF16),
        compiler_params=_cparams(("parallel", "parallel")),
        name="attn_sample",
    )(lam_vecs, subln_g, qb, k_past, v_past, kb, vb)


def _ssm_kernel(u_ref, h0_ref, a_ref, bh_ref, bl_ref, crh_ref, crl_ref, cih_ref, cil_ref,
                d_ref, wg_ref, bg_ref, s_ref, hout_ref, bu_ref, h_ref, *, steps, streams):
    n_state = a_ref.shape[1]
    n_blocks = bh_ref.shape[0]
    in_w = bh_ref.shape[1]
    st_w = n_state // n_blocks

    @pl.when(pl.program_id(0) == 0)
    def _():
        h_ref[...] = h0_ref[...]

    u = u_ref[...]
    u_hi, u_lo = _split_bf16(u)
    for gb in range(n_blocks):
        cols = slice(gb * in_w, (gb + 1) * in_w)
        bu = _dot3(u_hi[:, cols], u_lo[:, cols], bh_ref[gb], bl_ref[gb])
        bu_ref[:, gb * st_w:(gb + 1) * st_w] = bu[:, :st_w]
        bu_ref[:, n_state + gb * st_w:n_state + (gb + 1) * st_w] = bu[:, st_w:]

    for c in range(n_state // SCAN_COLS):
        re = slice(c * SCAN_COLS, (c + 1) * SCAN_COLS)
        im = slice(n_state + c * SCAN_COLS, n_state + (c + 1) * SCAN_COLS)
        ar = jnp.broadcast_to(a_ref[0:1, re], (streams, SCAN_COLS))
        ai = jnp.broadcast_to(a_ref[1:2, re], (streams, SCAN_COLS))

        def step(t, carry):
            hr, hi = carry
            rows = pl.ds(pl.multiple_of(t * streams, streams), streams)
            nr = ar * hr - ai * hi + bu_ref[rows, re]
            ni = ar * hi + ai * hr + bu_ref[rows, im]
            bu_ref[rows, re] = nr
            bu_ref[rows, im] = ni
            return nr, ni

        hr, hi = lax.fori_loop(0, steps, step, (h_ref[0, :, re], h_ref[1, :, re]), unroll=4)
        h_ref[0, :, re] = hr
        h_ref[1, :, re] = hi
    hout_ref[...] = h_ref[...]

    ys = []
    for gb in range(n_blocks):
        hr_hi, hr_lo = _split_bf16(bu_ref[:, gb * st_w:(gb + 1) * st_w])
        hi_hi, hi_lo = _split_bf16(bu_ref[:, n_state + gb * st_w:n_state + (gb + 1) * st_w])
        ys.append(_dot3(hr_hi, hr_lo, crh_ref[gb], crl_ref[gb]) + _dot3(hi_hi, hi_lo, cih_ref[gb], cil_ref[gb]))
    y = jax.nn.gelu(jnp.concatenate(ys, axis=1) + d_ref[...] * u)
    z = _dot(y.astype(BF16), wg_ref[...]) + bg_ref[...]
    s_ref[...] = (y * jax.nn.sigmoid(z)).astype(BF16)


def _ssm(u_tm, h0, a, bh, bl, crh, crl, cih, cil, d, wg, bg, *, streams, steps):
    rows_total, width = u_tm.shape
    rows = steps * streams
    n_state = a.shape[1]
    row = pl.BlockSpec((rows, width), lambda i: (i, 0))
    consts = [h0, a, bh, bl, crh, crl, cih, cil, d, wg, bg]
    return pl.pallas_call(
        functools.partial(_ssm_kernel, steps=steps, streams=streams),
        grid=(rows_total // rows,),
        in_specs=[row] + [_const_spec(c.shape) for c in consts],
        out_specs=[row, _const_spec(h0.shape)],
        out_shape=[jax.ShapeDtypeStruct((rows_total, width), BF16), jax.ShapeDtypeStruct(h0.shape, F32)],
        scratch_shapes=[pltpu.VMEM((rows, 2 * n_state), F32), pltpu.VMEM(h0.shape, F32)],
        compiler_params=_cparams(("arbitrary",)),
        name="ssm",
    )(u_tm, *consts)


def _ssm_params(a_re, a_im, log_dt, b_re, b_im, c_re, c_im):
    G, N, P = b_re.shape
    gb_n = SSM_GROUP_BLOCK
    nb = G // gb_n
    dt = jnp.exp(log_dt.astype(F32))[:, None]
    ar, ai = a_re.astype(F32), a_im.astype(F32)
    mag = jnp.exp(dt * ar)
    abar_r = mag * jnp.cos(dt * ai)
    abar_i = mag * jnp.sin(dt * ai)
    den = ar * ar + ai * ai
    nr, ni = abar_r - 1.0, abar_i
    coef_r = (nr * ar + ni * ai) / den
    coef_i = (ni * ar - nr * ai) / den
    br, bi = b_re.astype(F32), b_im.astype(F32)
    bbar_r = coef_r[..., None] * br - coef_i[..., None] * bi
    bbar_i = coef_r[..., None] * bi + coef_i[..., None] * br
    eye = jnp.eye(gb_n, dtype=F32)

    def in_block(bb):
        w = bb.reshape(nb, gb_n, N, P).transpose(0, 1, 3, 2)
        return jnp.einsum('bgpn,gh->bgphn', w, eye).reshape(nb, gb_n * P, gb_n * N)

    def out_block(cc):
        w = cc.reshape(nb, gb_n, P, N).transpose(0, 1, 3, 2)
        return jnp.einsum('bgnp,gh->bgnhp', w, eye).reshape(nb, gb_n * N, gb_n * P)

    b_bd = jnp.concatenate([in_block(bbar_r), in_block(bbar_i)], axis=2)
    a = jnp.stack([abar_r.reshape(-1), abar_i.reshape(-1)])
    return a, _split_bf16(b_bd), _split_bf16(out_block(c_re.astype(F32))), _split_bf16(-out_block(c_im.astype(F32)))


def _outproj_kernel(x_ref, o_ref, s_ref, wo_ref, g2_ref, wq_ref, keys_ref, x1_ref, xn_ref, sc_ref):
    aw = o_ref.shape[1]
    x1 = x_ref[...] + _dot(o_ref[...], wo_ref[:aw, :]) + _dot(s_ref[...], wo_ref[aw:, :])
    x1_ref[...] = x1
    ms = jnp.mean(x1 * x1, axis=-1, keepdims=True)
    xn = x1 * lax.rsqrt(ms + EPS) * g2_ref[...]
    xn_ref[...] = xn
    q = _dot(xn.astype(BF16), wq_ref[...])
    for j in range(keys_ref.shape[0]):
        qj = q[:, j * PEER_HALF:(j + 1) * PEER_HALF].astype(BF16)
        sc_ref[j] = _qk(keys_ref[j], qj)


def _outproj(x2d, o_b, s_b, wo_bf, g2, wq_bf, keys_bf, *, tm):
    T, D = x2d.shape
    nset = keys_bf.shape[0]
    row = lambda w: pl.BlockSpec((tm, w), lambda i: (i, 0))
    return pl.pallas_call(
        _outproj_kernel,
        grid=(T // tm,),
        in_specs=[row(D), row(o_b.shape[1]), row(s_b.shape[1]), _const_spec(wo_bf.shape), _const_spec((1, D)),
                  _const_spec(wq_bf.shape), _const_spec(keys_bf.shape)],
        out_specs=[row(D), row(D), pl.BlockSpec((nset, PEER_KEYS, tm), lambda i: (0, 0, i))],
        out_shape=[jax.ShapeDtypeStruct((T, D), F32), jax.ShapeDtypeStruct((T, D), F32),
                   jax.ShapeDtypeStruct((nset, PEER_KEYS, T), F32)],
        compiler_params=_cparams(("parallel",)),
        name="outproj",
    )(x2d, o_b, s_b, wo_bf, g2, wq_bf, keys_bf)


def _extract_topk(vals, payload, k):
    n, tt = vals.shape
    pos = lax.broadcasted_iota(jnp.int32, (n, tt), 0).astype(F32)
    slot = lax.broadcasted_iota(jnp.int32, (k, tt), 0)
    out_v = jnp.zeros((k, tt), F32)
    out_p = jnp.zeros((k, tt), F32)
    for r in range(k):
        m = jnp.max(vals, axis=0, keepdims=True)
        first = jnp.min(jnp.where(vals == m, pos, float(n)), axis=0, keepdims=True)
        hit = pos == first
        picked = jnp.max(jnp.where(hit, payload, -1.0), axis=0, keepdims=True)
        out_v = jnp.where(slot == r, m, out_v)
        out_p = jnp.where(slot == r, picked, out_p)
        vals = jnp.where(hit, -jnp.inf, vals)
    return out_v, out_p


def _topk_kernel(sc_ref, idx_ref, gate_ref):
    k = PEER_TOPK
    tt = sc_ref.shape[2]
    key_id = lax.broadcasted_iota(jnp.int32, (PEER_KEYS, tt), 0).astype(F32)

    def head(h, _):
        v0, i0 = _extract_topk(sc_ref[2 * h], key_id, k)
        v1, i1 = _extract_topk(sc_ref[2 * h + 1], key_id, k)
        cand = jnp.concatenate([v0[i:i + 1] + v1 for i in range(k)], axis=0)
        cidx = jnp.concatenate([i0[i:i + 1] * float(PEER_KEYS) + i1 for i in range(k)], axis=0)
        fv, fe = _extract_topk(cand, cidx, k)
        e = jnp.exp(fv - fv[0:1])
        rows = pl.ds(pl.multiple_of(h * k, k), k)
        gate_ref[rows, :] = e / jnp.sum(e, axis=0, keepdims=True)
        idx_ref[rows, :] = fe.astype(jnp.int32)
        return 0

    lax.fori_loop(0, PEER_HEADS, head, 0)


def _topk(scores):
    nset, nkeys, T = scores.shape
    tt = TOPK_TOKENS
    out = pl.BlockSpec((PEER_HEADS * PEER_TOPK, tt), lambda i: (0, i))
    return pl.pallas_call(
        _topk_kernel,
        grid=(T // tt,),
        in_specs=[pl.BlockSpec((nset, nkeys, tt), lambda i: (0, 0, i))],
        out_specs=[out, out],
        out_shape=[jax.ShapeDtypeStruct((PEER_HEADS * PEER_TOPK, T), jnp.int32),
                   jax.ShapeDtypeStruct((PEER_HEADS * PEER_TOPK, T), F32)],
        compiler_params=_cparams(("parallel",)),
        name="peer_topk",
    )(scores)


def _fold_pairs(vs, shift, keep):
    return [jnp.where(keep, a, b) + pltpu.roll(jnp.where(keep, b, a), shift, 0) for a, b in zip(vs[0::2], vs[1::2])]


def _expert_kernel(idx_ref, xs_ref, x1s_ref, gate_ref, uv_hbm, y_ref, buf, sem, cb_ref, *, tokens, slots):
    n_sel = buf.shape[1]
    half = SUBLANES
    lane = lax.broadcasted_iota(jnp.int32, (n_sel, tokens), 1)
    sub = lax.broadcasted_iota(jnp.int32, (SUBLANES, LANES), 0)
    keep1, keep2, keep4 = sub % 2 == 0, sub % 4 < 2, sub < 4

    def issue(t):
        slot = t % slots
        for k in range(n_sel):
            pltpu.make_async_copy(uv_hbm.at[idx_ref[t, k]], buf.at[slot, k], sem.at[slot]).start()

    def wait(t):
        slot = t % slots
        pltpu.make_async_copy(uv_hbm.at[pl.ds(0, n_sel)], buf.at[slot], sem.at[slot]).wait()

    def compute(t):
        slot = t % slots
        x = xs_ref[t]
        xa, xb = x[:half], x[half:]
        folded = []
        for g in range(n_sel // SUBLANES):
            ps = [buf[slot, k, 0, :half, :] * xa + buf[slot, k, 0, half:, :] * xb
                  for k in range(g * SUBLANES, (g + 1) * SUBLANES)]
            ps = _fold_pairs(_fold_pairs(_fold_pairs(ps, 1, keep1), 2, keep2), 4, keep4)
            folded.append(ps[0])
        act = jnp.sum(jnp.concatenate(folded, axis=0), axis=1, keepdims=True)
        gate = jnp.sum(jnp.where(lane == t, gate_ref[...], 0.0), axis=1, keepdims=True)
        cb_ref[...] = jnp.broadcast_to(gate * jax.nn.gelu(act), cb_ref.shape)
        n_acc = 4
        acc_a = [jnp.zeros((half, LANES), F32)] * n_acc
        acc_b = [jnp.zeros((half, LANES), F32)] * n_acc
        for k in range(n_sel):
            c = jnp.broadcast_to(cb_ref[pl.ds(k, 1), :], (half, LANES))
            acc_a[k % n_acc] = acc_a[k % n_acc] + c * buf[slot, k, 1, :half, :]
            acc_b[k % n_acc] = acc_b[k % n_acc] + c * buf[slot, k, 1, half:, :]
        out = jnp.concatenate([(acc_a[0] + acc_a[1]) + (acc_a[2] + acc_a[3]),
                               (acc_b[0] + acc_b[1]) + (acc_b[2] + acc_b[3])], axis=0)
        y_ref[t] = x1s_ref[t] + out

    ahead = slots - 1

    def fill(t, c):
        issue(t)
        return c

    def steady(t, c):
        issue(t + ahead)
        wait(t)
        compute(t)
        return c

    def drain(t, c):
        wait(t)
        compute(t)
        return c

    lax.fori_loop(0, ahead, fill, 0)
    lax.fori_loop(0, tokens - ahead, steady, 0)
    lax.fori_loop(tokens - ahead, tokens, drain, 0)


def _experts(idx_tok, xn, x1, gates, uv):
    T, D = xn.shape
    n_sel = idx_tok.shape[1]
    tt = EXPERT_TOKENS
    rows = D // LANES
    slab = pl.BlockSpec((tt, rows, LANES), lambda i: (i, 0, 0))
    y = pl.pallas_call(
        functools.partial(_expert_kernel, tokens=tt, slots=EXPERT_SLOTS),
        grid=(T // tt,),
        in_specs=[pl.BlockSpec((tt, n_sel), lambda i: (i, 0), memory_space=pltpu.SMEM), slab, slab,
                  pl.BlockSpec((n_sel, tt), lambda i: (0, i)), pl.BlockSpec(memory_space=pl.ANY)],
        out_specs=slab,
        out_shape=jax.ShapeDtypeStruct((T, rows, LANES), F32),
        scratch_shapes=[pltpu.VMEM((EXPERT_SLOTS, n_sel) + uv.shape[1:], F32),
                        pltpu.SemaphoreType.DMA((EXPERT_SLOTS,)), pltpu.VMEM((n_sel, LANES), F32)],
        compiler_params=_cparams(("arbitrary",)),
        name="peer_experts",
    )(idx_tok, xn.reshape(T, rows, LANES), x1.reshape(T, rows, LANES), gates, uv)
    return y.reshape(T, D)


def _rope_tables(pos):
    half = ROPE_DIM // 2
    inv_freq = ROPE_THETA ** (-jnp.arange(half, dtype=F32) * (2.0 / ROPE_DIM))
    ang = pos.astype(F32)[:, None] * inv_freq[None, :]
    cos, sin = jnp.cos(ang), jnp.sin(ang)
    n = pos.shape[0]
    pad = jnp.zeros((n, ATT_HD - ROPE_DIM), F32)
    zero = jnp.zeros((n, half), F32)
    c = jnp.concatenate([cos, cos, pad + 1.0], axis=1)
    sa = jnp.concatenate([-sin, zero, pad], axis=1)
    sb = jnp.concatenate([zero, sin, pad], axis=1)
    return tuple(jnp.tile(t, (1, LANES // ATT_HD)) for t in (c, sa, sb))


def _layer(x, pos, past_k, past_v, h0_re, h0_im, lam_init, p):
    B, L, D = x.shape
    T = B * L
    x2d = x.reshape(T, D)
    tm = min(INPROJ_ROWS, L)
    cos, sa, sb = _rope_tables(pos)
    qb, kf, kb, vf, vb, u = _inproj(x2d, p['g1'], p['w_in'], p['bd'], p['qg'], p['kg'], cos, sa, sb,
                                   tm=tm, pos_blocks=L // tm)
    if past_k is None:
        o = _attn_prompt(p['lam'], p['subln'], qb, kb, vb, batch=B, seq=L, lam_init=lam_init)
    else:
        past = past_k.shape[1]
        o = _attn_sample(p['lam'], p['subln'], qb, past_k.reshape(B * past, ATT_WIDTH),
                         past_v.reshape(B * past, ATT_WIDTH), kb, vb, batch=B, seq=L, past=past,
                         lam_init=lam_init)
    n_state = p['a'].shape[1]
    if h0_re is None:
        h0 = jnp.zeros((2, B, n_state), F32)
    else:
        h0 = jnp.stack([h0_re.reshape(B, n_state), h0_im.reshape(B, n_state)]).astype(F32)
    ssm_w = u.shape[1]
    u_tm = u.reshape(B, L, ssm_w).transpose(1, 0, 2).reshape(T, ssm_w)
    s_tm, h_fin = _ssm(u_tm, h0, p['a'], *p['b'], *p['cr'], *p['ci'], p['d'], p['wg'], p['bg'],
                       streams=B, steps=min(SSM_STEPS, L))
    s = s_tm.reshape(L, B, ssm_w).transpose(1, 0, 2).reshape(T, ssm_w)
    x1, xn, scores = _outproj(x2d, o, s, p['w_out'], p['g2'], p['wq'], p['keys'], tm=min(OUT_ROWS, T))
    idx_t, gates = _topk(scores)
    y = _experts(idx_t.T, xn, x1, gates, p['uv'])
    G = n_state // SSM_N
    return (y.reshape(B, L, D), kf.reshape(B, L, ATT_HEADS, 2, ATT_HD), vf.reshape(B, L, ATT_HEADS, HEAD_W),
            h_fin[0].reshape(B, G, SSM_N), h_fin[1].reshape(B, G, SSM_N))


def kernel(x_prompt, x_sample, cache_k, cache_v, state_ssm_re, state_ssm_im, norm1_g, w_in, q_norm_g, k_norm_g, lambda_q1, lambda_k1, lambda_q2, lambda_k2, subln_g, ssm_a_re, ssm_a_im, ssm_log_dt, ssm_b_re, ssm_b_im, ssm_c_re, ssm_c_im, ssm_d, w_glu, b_glu, w_out, norm2_g, w_peer_q, peer_keys, peer_u, peer_v):
    depth = w_in.shape[0]
    Lp, Ls, past = x_prompt.shape[1], x_sample.shape[1], cache_k.shape[2]
    pos_p = jnp.arange(Lp, dtype=jnp.int32)
    pos_s = past + jnp.arange(Ls, dtype=jnp.int32)
    lane = jnp.arange(LANES)
    bd = (lane[:, None] // ATT_HD == lane[None, :] // ATT_HD).astype(BF16)
    yp, ys = x_prompt, x_sample
    outs = [[] for _ in range(8)]
    for layer in range(depth):
        lam_init = 0.8 - 0.6 * math.exp(-0.3 * layer)
        a, b, cr, ci = _ssm_params(ssm_a_re[layer], ssm_a_im[layer], ssm_log_dt[layer], ssm_b_re[layer],
                                   ssm_b_im[layer], ssm_c_re[layer], ssm_c_im[layer])
        slab_rows = peer_u.shape[-1] // LANES
        p = dict(
            g1=norm1_g[layer][None].astype(F32), w_in=w_in[layer].astype(BF16), bd=bd,
            qg=jnp.tile(q_norm_g[layer].astype(F32), LANES // ATT_HD)[None],
            kg=jnp.tile(k_norm_g[layer].astype(F32), LANES // ATT_HD)[None],
            lam=jnp.stack([lambda_q1[layer], lambda_k1[layer], lambda_q2[layer], lambda_k2[layer]]).astype(F32),
            subln=subln_g[layer][None].astype(F32),
            a=a, b=b, cr=cr, ci=ci, d=ssm_d[layer][None].astype(F32), wg=w_glu[layer].astype(BF16),
            bg=b_glu[layer][None].astype(F32), w_out=w_out[layer].astype(BF16),
            g2=norm2_g[layer][None].astype(F32), wq=w_peer_q[layer].astype(BF16),
            keys=peer_keys[layer].reshape(PEER_HEADS * 2, PEER_KEYS, PEER_HALF).astype(BF16),
            uv=jnp.stack([peer_u[layer].reshape(-1, slab_rows, LANES),
                          peer_v[layer].reshape(-1, slab_rows, LANES)], axis=1))
        yp, kp, vp, hrp, hip = _layer(yp, pos_p, None, None, None, None, lam_init, p)
        ys, kk, vv, hrs, his = _layer(ys, pos_s, cache_k[layer], cache_v[layer], state_ssm_re[layer],
                                      state_ssm_im[layer], lam_init, p)
        for lst, val in zip(outs, (kp, vp, hrp, hip, kk, vv, hrs, his)):
            lst.append(val)
    return (yp, ys) + tuple(jnp.stack(l) for l in outs)
```

```python
import functools
import math

import jax
import jax.numpy as jnp
from jax import lax
from jax.experimental import pallas as pl
from jax.experimental.pallas import tpu as pltpu

F32 = jnp.float32
BF16 = jnp.bfloat16

CHUNK = 64
ATT_HEADS = 8
ATT_HD = 64
HEAD_W = 2 * ATT_HD
ATT_WIDTH = ATT_HEADS * HEAD_W
ROPE_DIM = ATT_HD // 4
ROPE_THETA = 500000.0
SSM_P = 16
SSM_N = 64
PEER_HEADS = 8
PEER_KEYS = 128
PEER_HALF = 128
PEER_TOPK = 16
EPS = 1e-6
NEG_INF = -1e30

LANES = 128
SUBLANES = 8
VMEM_LIMIT_BYTES = 56 * 1024 * 1024

INPROJ_ROWS = 256
ATTN_Q_ROWS = 256
SSM_STEPS = 32
SSM_GROUP_BLOCK = 16
SCAN_COLS = 512
OUT_ROWS = 256
TOPK_TOKENS = 128
EXPERT_TOKENS = 128
EXPERT_SLOTS = 8


def _cparams(sem):
    return pltpu.CompilerParams(dimension_semantics=sem, vmem_limit_bytes=VMEM_LIMIT_BYTES)


def _const_spec(shape):
    nd = len(shape)
    return pl.BlockSpec(shape, lambda *_: (0,) * nd, pipeline_mode=pl.Buffered(1))


def _split_bf16(a):
    hi = a.astype(BF16)
    lo = (a - hi.astype(F32)).astype(BF16)
    return hi, lo


def _dot(a, b):
    return jnp.dot(a, b, preferred_element_type=F32)


def _dot3(a_hi, a_lo, w_hi, w_lo):
    return _dot(a_hi, w_hi) + _dot(a_lo, w_hi) + _dot(a_hi, w_lo)


def _inproj_kernel(x_ref, g1_ref, w_ref, bd_ref, qg_ref, kg_ref, cos_ref, sa_ref, sb_ref,
                   qb_ref, kf_ref, kb_ref, vf_ref, vb_ref, u_ref):
    x = x_ref[...]
    ms = jnp.mean(x * x, axis=-1, keepdims=True)
    xn = (x * lax.rsqrt(ms + EPS) * g1_ref[...]).astype(BF16)
    proj = _dot(xn, w_ref[...])
    bd = bd_ref[...]
    cos, sa, sb = cos_ref[...], sa_ref[...], sb_ref[...]

    def head_norm_rope(t, g):
        hi, lo = _split_bf16(t * t)
        ssq = _dot(hi, bd) + _dot(lo, bd)
        n = t * lax.rsqrt(ssq * (1.0 / ATT_HD) + EPS) * g
        return n * cos + pltpu.roll(n, LANES - ROPE_DIM // 2, 1) * sa + pltpu.roll(n, ROPE_DIM // 2, 1) * sb

    for h in range(ATT_HEADS):
        c0 = h * HEAD_W
        q = head_norm_rope(proj[:, c0:c0 + HEAD_W], qg_ref[...])
        qb_ref[:, c0:c0 + HEAD_W] = (q * (ATT_HD ** -0.5)).astype(BF16)
        k = head_norm_rope(proj[:, ATT_WIDTH + c0:ATT_WIDTH + c0 + HEAD_W], kg_ref[...])
        kf_ref[:, c0:c0 + HEAD_W] = k
        kb_ref[:, c0:c0 + HEAD_W] = k.astype(BF16)
    v = proj[:, 2 * ATT_WIDTH:3 * ATT_WIDTH]
    vf_ref[...] = v
    vb_ref[...] = v.astype(BF16)
    u_ref[...] = proj[:, 3 * ATT_WIDTH:]


def _inproj(x2d, g1, w_bf, bd, qg, kg, cos, sa, sb, *, tm, pos_blocks):
    T, D = x2d.shape
    n_in = w_bf.shape[1]
    ssm_w = n_in - 3 * ATT_WIDTH
    row = lambda w: pl.BlockSpec((tm, w), lambda i: (i, 0))
    pos = pl.BlockSpec((tm, LANES), lambda i: (i % pos_blocks, 0))
    outs = [jax.ShapeDtypeStruct((T, ATT_WIDTH), BF16), jax.ShapeDtypeStruct((T, ATT_WIDTH), F32),
            jax.ShapeDtypeStruct((T, ATT_WIDTH), BF16), jax.ShapeDtypeStruct((T, ATT_WIDTH), F32),
            jax.ShapeDtypeStruct((T, ATT_WIDTH), BF16), jax.ShapeDtypeStruct((T, ssm_w), F32)]
    return pl.pallas_call(
        _inproj_kernel,
        grid=(T // tm,),
        in_specs=[row(D), _const_spec((1, D)), _const_spec((D, n_in)), _const_spec((LANES, LANES)),
                  _const_spec((1, LANES)), _const_spec((1, LANES)), pos, pos, pos],
        out_specs=[row(ATT_WIDTH)] * 5 + [row(ssm_w)],
        out_shape=outs,
        compiler_params=_cparams(("parallel",)),
        name="inproj",
    )(x2d, g1, w_bf, bd, qg, kg, cos, sa, sb)


def _diff_lambda(lam_ref, lam_init):
    lv = lam_ref[...]
    a = jnp.sum(lv[0:1] * lv[1:2], axis=-1, keepdims=True)
    b = jnp.sum(lv[2:3] * lv[3:4], axis=-1, keepdims=True)
    return jnp.exp(a) - jnp.exp(b) + lam_init


def _stack_maps(q):
    lane = lax.broadcasted_iota(jnp.int32, q.shape, 1)
    zero = jnp.zeros_like(q)
    return jnp.concatenate([jnp.where(lane < ATT_HD, q, zero), jnp.where(lane >= ATT_HD, q, zero)], axis=0)


def _attn_finish(acc, l, tq, lam, g, lam_init):
    o = acc[:tq] / l[:tq] - lam * (acc[tq:] / l[tq:])
    ms = jnp.mean(o * o, axis=-1, keepdims=True)
    return (o * lax.rsqrt(ms + EPS) * g * (1.0 - lam_init)).astype(BF16)


def _qk(qq, kb):
    return lax.dot_general(qq, kb, (((1,), (1,)), ((), ())), preferred_element_type=F32)


def _attn_prompt_kernel(lam_ref, g_ref, q_ref, k_ref, v_ref, o_ref, *, tq, lam_init):
    i = pl.program_id(2)
    qq = _stack_maps(q_ref[...])

    def update(carry, s, vb):
        m, l, acc = carry
        m_new = jnp.maximum(m, jnp.max(s, axis=1, keepdims=True))
        alpha = jnp.exp(m - m_new)
        p = jnp.exp(s - m_new)
        l = alpha * l + jnp.sum(p, axis=1, keepdims=True)
        acc = alpha * acc + _dot(p.astype(BF16), vb)
        return m_new, l, acc

    def full_block(j, carry):
        rows = pl.ds(pl.multiple_of(j * tq, tq), tq)
        return update(carry, _qk(qq, k_ref[rows, :]), v_ref[rows, :])

    carry = (jnp.full((2 * tq, 1), NEG_INF, F32), jnp.zeros((2 * tq, 1), F32),
             jnp.zeros((2 * tq, HEAD_W), F32))
    carry = lax.fori_loop(0, i, full_block, carry)
    rows = pl.ds(pl.multiple_of(i * tq, tq), tq)
    s = _qk(qq, k_ref[rows, :])
    r = lax.broadcasted_iota(jnp.int32, s.shape, 0) % tq
    c = lax.broadcasted_iota(jnp.int32, s.shape, 1)
    s = jnp.where(c // CHUNK <= r // CHUNK, s, NEG_INF)
    _, l, acc = update(carry, s, v_ref[rows, :])
    o_ref[...] = _attn_finish(acc, l, tq, _diff_lambda(lam_ref, lam_init), g_ref[...], lam_init)


def _attn_prompt(lam_vecs, subln_g, qb, kb, vb, *, batch, seq, lam_init):
    tq = ATTN_Q_ROWS
    nq = seq // tq
    T = batch * seq
    qspec = pl.BlockSpec((tq, HEAD_W), lambda b, h, i: (b * nq + i, h))
    kvspec = pl.BlockSpec((seq, HEAD_W), lambda b, h, i: (b, h))
    return pl.pallas_call(
        functools.partial(_attn_prompt_kernel, tq=tq, lam_init=lam_init),
        grid=(batch, ATT_HEADS, nq),
        in_specs=[_const_spec((4, ATT_HD)), _const_spec((1, HEAD_W)), qspec, kvspec, kvspec],
        out_specs=qspec,
        out_shape=jax.ShapeDtypeStruct((T, ATT_WIDTH), BF16),
        compiler_params=_cparams(("parallel", "parallel", "arbitrary")),
        name="attn_prompt",
    )(lam_vecs, subln_g, qb, kb, vb)


def _attn_sample_kernel(lam_ref, g_ref, q_ref, kp_ref, vp_ref, kn_ref, vn_ref, o_ref, *, tq, lam_init):
    qq = _stack_maps(q_ref[...])
    s_past = _qk(qq, kp_ref[...].astype(BF16))
    s_new = _qk(qq, kn_ref[...])
    m = jnp.maximum(jnp.max(s_past, axis=1, keepdims=True), jnp.max(s_new, axis=1, keepdims=True))
    p_past = jnp.exp(s_past - m)
    p_new = jnp.exp(s_new - m)
    l = jnp.sum(p_past, axis=1, keepdims=True) + jnp.sum(p_new, axis=1, keepdims=True)
    acc = _dot(p_past.astype(BF16), vp_ref[...].astype(BF16)) + _dot(p_new.astype(BF16), vn_ref[...])
    o_ref[...] = _attn_finish(acc, l, tq, _diff_lambda(lam_ref, lam_init), g_ref[...], lam_init)


def _attn_sample(lam_vecs, subln_g, qb, k_past, v_past, kb, vb, *, batch, seq, past, lam_init):
    new = pl.BlockSpec((seq, HEAD_W), lambda b, h: (b, h))
    old = pl.BlockSpec((past, HEAD_W), lambda b, h: (b, h))
    return pl.pallas_call(
        functools.partial(_attn_sample_kernel, tq=seq, lam_init=lam_init),
        grid=(batch, ATT_HEADS),
        in_specs=[_const_spec((4, ATT_HD)), _const_spec((1, HEAD_W)), new, old, old, new, new],
        out_specs=new,
        out_shape=jax.ShapeDtypeStruct((batch * seq, ATT_WIDTH), BF16),
        compiler_params=_cparams(("parallel", "parallel")),
        name="attn_sample",
    )(lam_vecs, subln_g, qb, k_past, v_past, kb, vb)


def _ssm_kernel(u_ref, h0_ref, a_ref, bh_ref, bl_ref, crh_ref, crl_ref, cih_ref, cil_ref,
                d_ref, wg_ref, bg_ref, s_ref, hout_ref, bu_ref, h_ref, *, steps, streams):
    n_state = a_ref.shape[1]
    n_blocks = bh_ref.shape[0]
    in_w = bh_ref.shape[1]
    st_w = n_state // n_blocks

    @pl.when(pl.program_id(0) == 0)
    def _():
        h_ref[...] = h0_ref[...]

    u = u_ref[...]
    u_hi, u_lo = _split_bf16(u)
    for gb in range(n_blocks):
        cols = slice(gb * in_w, (gb + 1) * in_w)
        bu = _dot3(u_hi[:, cols], u_lo[:, cols], bh_ref[gb], bl_ref[gb])
        bu_ref[:, gb * st_w:(gb + 1) * st_w] = bu[:, :st_w]
        bu_ref[:, n_state + gb * st_w:n_state + (gb + 1) * st_w] = bu[:, st_w:]

    for c in range(n_state // SCAN_COLS):
        re = slice(c * SCAN_COLS, (c + 1) * SCAN_COLS)
        im = slice(n_state + c * SCAN_COLS, n_state + (c + 1) * SCAN_COLS)
        ar = jnp.broadcast_to(a_ref[0:1, re], (streams, SCAN_COLS))
        ai = jnp.broadcast_to(a_ref[1:2, re], (streams, SCAN_COLS))

        def step(t, carry):
            hr, hi = carry
            rows = pl.ds(pl.multiple_of(t * streams, streams), streams)
            nr = ar * hr - ai * hi + bu_ref[rows, re]
            ni = ar * hi + ai * hr + bu_ref[rows, im]
            bu_ref[rows, re] = nr
            bu_ref[rows, im] = ni
            return nr, ni

        hr, hi = lax.fori_loop(0, steps, step, (h_ref[0, :, re], h_ref[1, :, re]), unroll=4)
        h_ref[0, :, re] = hr
        h_ref[1, :, re] = hi
    hout_ref[...] = h_ref[...]

    ys = []
    for gb in range(n_blocks):
        hr_hi, hr_lo = _split_bf16(bu_ref[:, gb * st_w:(gb + 1) * st_w])
        hi_hi, hi_lo = _split_bf16(bu_ref[:, n_state + gb * st_w:n_state + (gb + 1) * st_w])
        ys.append(_dot3(hr_hi, hr_lo, crh_ref[gb], crl_ref[gb]) + _dot3(hi_hi, hi_lo, cih_ref[gb], cil_ref[gb]))
    y = jax.nn.gelu(jnp.concatenate(ys, axis=1) + d_ref[...] * u)
    z = _dot(y.astype(BF16), wg_ref[...]) + bg_ref[...]
    s_ref[...] = (y * jax.nn.sigmoid(z)).astype(BF16)


def _ssm(u_tm, h0, a, bh, bl, crh, crl, cih, cil, d, wg, bg, *, streams, steps):
    rows_total, width = u_tm.shape
    rows = steps * streams
    n_state = a.shape[1]
    row = pl.BlockSpec((rows, width), lambda i: (i, 0))
    consts = [h0, a, bh, bl, crh, crl, cih, cil, d, wg, bg]
    return pl.pallas_call(
        functools.partial(_ssm_kernel, steps=steps, streams=streams),
        grid=(rows_total // rows,),
        in_specs=[row] + [_const_spec(c.shape) for c in consts],
        out_specs=[row, _const_spec(h0.shape)],
        out_shape=[jax.ShapeDtypeStruct((rows_total, width), BF16), jax.ShapeDtypeStruct(h0.shape, F32)],
        scratch_shapes=[pltpu.VMEM((rows, 2 * n_state), F32), pltpu.VMEM(h0.shape, F32)],
        compiler_params=_cparams(("arbitrary",)),
        name="ssm",
    )(u_tm, *consts)


def _ssm_params(a_re, a_im, log_dt, b_re, b_im, c_re, c_im):
    G, N, P = b_re.shape
    gb_n = SSM_GROUP_BLOCK
    nb = G // gb_n
    dt = jnp.exp(log_dt.astype(F32))[:, None]
    ar, ai = a_re.astype(F32), a_im.astype(F32)
    mag = jnp.exp(dt * ar)
    abar_r = mag * jnp.cos(dt * ai)
    abar_i = mag * jnp.sin(dt * ai)
    den = ar * ar + ai * ai
    nr, ni = abar_r - 1.0, abar_i
    coef_r = (nr * ar + ni * ai) / den
    coef_i = (ni * ar - nr * ai) / den
    br, bi = b_re.astype(F32), b_im.astype(F32)
    bbar_r = coef_r[..., None] * br - coef_i[..., None] * bi
    bbar_i = coef_r[..., None] * bi + coef_i[..., None] * br
    eye = jnp.eye(gb_n, dtype=F32)

    def in_block(bb):
        w = bb.reshape(nb, gb_n, N, P).transpose(0, 1, 3, 2)
        return jnp.einsum('bgpn,gh->bgphn', w, eye).reshape(nb, gb_n * P, gb_n * N)

    def out_block(cc):
        w = cc.reshape(nb, gb_n, P, N).transpose(0, 1, 3, 2)
        return jnp.einsum('bgnp,gh->bgnhp', w, eye).reshape(nb, gb_n * N, gb_n * P)

    b_bd = jnp.concatenate([in_block(bbar_r), in_block(bbar_i)], axis=2)
    a = jnp.stack([abar_r.reshape(-1), abar_i.reshape(-1)])
    return a, _split_bf16(b_bd), _split_bf16(out_block(c_re.astype(F32))), _split_bf16(-out_block(c_im.astype(F32)))


def _outproj_kernel(x_ref, o_ref, s_ref, wo_ref, g2_ref, wq_ref, keys_ref, x1_ref, xn_ref, sc_ref):
    aw = o_ref.shape[1]
    x1 = x_ref[...] + _dot(o_ref[...], wo_ref[:aw, :]) + _dot(s_ref[...], wo_ref[aw:, :])
    x1_ref[...] = x1
    ms = jnp.mean(x1 * x1, axis=-1, keepdims=True)
    xn = x1 * lax.rsqrt(ms + EPS) * g2_ref[...]
    xn_ref[...] = xn
    q = _dot(xn.astype(BF16), wq_ref[...])
    for j in range(keys_ref.shape[0]):
        qj = q[:, j * PEER_HALF:(j + 1) * PEER_HALF].astype(BF16)
        sc_ref[j] = _qk(keys_ref[j], qj)


def _outproj(x2d, o_b, s_b, wo_bf, g2, wq_bf, keys_bf, *, tm):
    T, D = x2d.shape
    nset = keys_bf.shape[0]
    row = lambda w: pl.BlockSpec((tm, w), lambda i: (i, 0))
    return pl.pallas_call(
        _outproj_kernel,
        grid=(T // tm,),
        in_specs=[row(D), row(o_b.shape[1]), row(s_b.shape[1]), _const_spec(wo_bf.shape), _const_spec((1, D)),
                  _const_spec(wq_bf.shape), _const_spec(keys_bf.shape)],
        out_specs=[row(D), row(D), pl.BlockSpec((nset, PEER_KEYS, tm), lambda i: (0, 0, i))],
        out_shape=[jax.ShapeDtypeStruct((T, D), F32), jax.ShapeDtypeStruct((T, D), F32),
                   jax.ShapeDtypeStruct((nset, PEER_KEYS, T), F32)],
        compiler_params=_cparams(("parallel",)),
        name="outproj",
    )(x2d, o_b, s_b, wo_bf, g2, wq_bf, keys_bf)


def _extract_topk(vals, payload, k):
    n, tt = vals.shape
    pos = lax.broadcasted_iota(jnp.int32, (n, tt), 0).astype(F32)
    slot = lax.broadcasted_iota(jnp.int32, (k, tt), 0)
    out_v = jnp.zeros((k, tt), F32)
    out_p = jnp.zeros((k, tt), F32)
    for r in range(k):
        m = jnp.max(vals, axis=0, keepdims=True)
        first = jnp.min(jnp.where(vals == m, pos, float(n)), axis=0, keepdims=True)
        hit = pos == first
        picked = jnp.max(jnp.where(hit, payload, -1.0), axis=0, keepdims=True)
        out_v = jnp.where(slot == r, m, out_v)
        out_p = jnp.where(slot == r, picked, out_p)
        vals = jnp.where(hit, -jnp.inf, vals)
    return out_v, out_p


def _topk_kernel(sc_ref, idx_ref, gate_ref):
    k = PEER_TOPK
    tt = sc_ref.shape[2]
    key_id = lax.broadcasted_iota(jnp.int32, (PEER_KEYS, tt), 0).astype(F32)

    def head(h, _):
        v0, i0 = _extract_topk(sc_ref[2 * h], key_id, k)
        v1, i1 = _extract_topk(sc_ref[2 * h + 1], key_id, k)
        cand = jnp.concatenate([v0[i:i + 1] + v1 for i in range(k)], axis=0)
        cidx = jnp.concatenate([i0[i:i + 1] * float(PEER_KEYS) + i1 for i in range(k)], axis=0)
        fv, fe = _extract_topk(cand, cidx, k)
        e = jnp.exp(fv - fv[0:1])
        rows = pl.ds(pl.multiple_of(h * k, k), k)
        gate_ref[rows, :] = e / jnp.sum(e, axis=0, keepdims=True)
        idx_ref[rows, :] = fe.astype(jnp.int32)
        return 0

    lax.fori_loop(0, PEER_HEADS, head, 0)


def _topk(scores):
    nset, nkeys, T = scores.shape
    tt = TOPK_TOKENS
    out = pl.BlockSpec((PEER_HEADS * PEER_TOPK, tt), lambda i: (0, i))
    return pl.pallas_call(
        _topk_kernel,
        grid=(T // tt,),
        in_specs=[pl.BlockSpec((nset, nkeys, tt), lambda i: (0, 0, i))],
        out_specs=[out, out],
        out_shape=[jax.ShapeDtypeStruct((PEER_HEADS * PEER_TOPK, T), jnp.int32),
                   jax.ShapeDtypeStruct((PEER_HEADS * PEER_TOPK, T), F32)],
        compiler_params=_cparams(("parallel",)),
        name="peer_topk",
    )(scores)


def _fold_pairs(vs, shift, keep):
    return [jnp.where(keep, a, b) + pltpu.roll(jnp.where(keep, b, a), shift, 0) for a, b in zip(vs[0::2], vs[1::2])]


def _expert_kernel(idx_ref, xs_ref, x1s_ref, gate_ref, uv_hbm, y_ref, buf, sem, cb_ref, *, tokens, slots):
    n_sel = buf.shape[1]
    half = SUBLANES
    lane = lax.broadcasted_iota(jnp.int32, (n_sel, tokens), 1)
    sub = lax.broadcasted_iota(jnp.int32, (SUBLANES, LANES), 0)
    keep1, keep2, keep4 = sub % 2 == 0, sub % 4 < 2, sub < 4

    def issue(t, ks):
        slot = t % slots
        for k in ks:
            pltpu.make_async_copy(uv_hbm.at[idx_ref[t, k]], buf.at[slot, k], sem.at[slot]).start()

    def wait(t):
        slot = t % slots
        pltpu.make_async_copy(uv_hbm.at[pl.ds(0, n_sel)], buf.at[slot], sem.at[slot]).wait()

    n_groups = n_sel // SUBLANES
    per_chunk = n_sel // (2 * n_groups)

    def compute(t, t_next):
        slot = t % slots
        x = xs_ref[t]
        xa, xb = x[:half], x[half:]
        folded = []
        for g in range(n_groups):
            if t_next is not None:
                issue(t_next, range(g * per_chunk, (g + 1) * per_chunk))
            ps = []
            for k in range(g * SUBLANES, (g + 1) * SUBLANES):
                w = buf[slot, k, 0].astype(F32)
                ps.append(w[:half] * xa + w[half:] * xb)
            ps = _fold_pairs(_fold_pairs(_fold_pairs(ps, 1, keep1), 2, keep2), 4, keep4)
            folded.append(ps[0])
        act = jnp.sum(jnp.concatenate(folded, axis=0), axis=1, keepdims=True)
        gate = jnp.sum(jnp.where(lane == t, gate_ref[...], 0.0), axis=1, keepdims=True)
        cb_ref[...] = jnp.broadcast_to(gate * jax.nn.gelu(act), cb_ref.shape)
        n_acc = 4
        acc_a = [jnp.zeros((half, LANES), F32)] * n_acc
        acc_b = [jnp.zeros((half, LANES), F32)] * n_acc
        for g in range(n_groups):
            if t_next is not None:
                issue(t_next, range((n_groups + g) * per_chunk, (n_groups + g + 1) * per_chunk))
            for k in range(g * SUBLANES, (g + 1) * SUBLANES):
                c = jnp.broadcast_to(cb_ref[pl.ds(k, 1), :], (half, LANES))
                w = buf[slot, k, 1].astype(F32)
                acc_a[k % n_acc] = acc_a[k % n_acc] + c * w[:half]
                acc_b[k % n_acc] = acc_b[k % n_acc] + c * w[half:]
        out = jnp.concatenate([(acc_a[0] + acc_a[1]) + (acc_a[2] + acc_a[3]),
                               (acc_b[0] + acc_b[1]) + (acc_b[2] + acc_b[3])], axis=0)
        y_ref[t] = x1s_ref[t] + out

    ahead = slots - 1

    def fill(t, c):
        issue(t, range(n_sel))
        return c

    def steady(t, c):
        wait(t)
        compute(t, t + ahead)
        return c

    def drain(t, c):
        wait(t)
        compute(t, None)
        return c

    lax.fori_loop(0, ahead, fill, 0)
    lax.fori_loop(0, tokens - ahead, steady, 0)
    lax.fori_loop(tokens - ahead, tokens, drain, 0)


def _experts(idx_tok, xn, x1, gates, uv):
    T, D = xn.shape
    n_sel = idx_tok.shape[1]
    tt = EXPERT_TOKENS
    rows = D // LANES
    slab = pl.BlockSpec((tt, rows, LANES), lambda i: (i, 0, 0))
    y = pl.pallas_call(
        functools.partial(_expert_kernel, tokens=tt, slots=EXPERT_SLOTS),
        grid=(T // tt,),
        in_specs=[pl.BlockSpec((tt, n_sel), lambda i: (i, 0), memory_space=pltpu.SMEM), slab, slab,
                  pl.BlockSpec((n_sel, tt), lambda i: (0, i)), pl.BlockSpec(memory_space=pl.ANY)],
        out_specs=slab,
        out_shape=jax.ShapeDtypeStruct((T, rows, LANES), F32),
        scratch_shapes=[pltpu.VMEM((EXPERT_SLOTS, n_sel) + uv.shape[1:], uv.dtype),
                        pltpu.SemaphoreType.DMA((EXPERT_SLOTS,)), pltpu.VMEM((n_sel, LANES), F32)],
        compiler_params=_cparams(("arbitrary",)),
        name="peer_experts",
    )(idx_tok, xn.reshape(T, rows, LANES), x1.reshape(T, rows, LANES), gates, uv)
    return y.reshape(T, D)


def _rope_tables(pos):
    half = ROPE_DIM // 2
    inv_freq = ROPE_THETA ** (-jnp.arange(half, dtype=F32) * (2.0 / ROPE_DIM))
    ang = pos.astype(F32)[:, None] * inv_freq[None, :]
    cos, sin = jnp.cos(ang), jnp.sin(ang)
    n = pos.shape[0]
    pad = jnp.zeros((n, ATT_HD - ROPE_DIM), F32)
    zero = jnp.zeros((n, half), F32)
    c = jnp.concatenate([cos, cos, pad + 1.0], axis=1)
    sa = jnp.concatenate([-sin, zero, pad], axis=1)
    sb = jnp.concatenate([zero, sin, pad], axis=1)
    return tuple(jnp.tile(t, (1, LANES // ATT_HD)) for t in (c, sa, sb))


def _layer(x, pos, past_k, past_v, h0_re, h0_im, lam_init, p):
    B, L, D = x.shape
    T = B * L
    x2d = x.reshape(T, D)
    tm = min(INPROJ_ROWS, L)
    cos, sa, sb = _rope_tables(pos)
    qb, kf, kb, vf, vb, u = _inproj(x2d, p['g1'], p['w_in'], p['bd'], p['qg'], p['kg'], cos, sa, sb,
                                   tm=tm, pos_blocks=L // tm)
    if past_k is None:
        o = _attn_prompt(p['lam'], p['subln'], qb, kb, vb, batch=B, seq=L, lam_init=lam_init)
    else:
        past = past_k.shape[1]
        o = _attn_sample(p['lam'], p['subln'], qb, past_k.reshape(B * past, ATT_WIDTH),
                         past_v.reshape(B * past, ATT_WIDTH), kb, vb, batch=B, seq=L, past=past,
                         lam_init=lam_init)
    n_state = p['a'].shape[1]
    if h0_re is None:
        h0 = jnp.zeros((2, B, n_state), F32)
    else:
        h0 = jnp.stack([h0_re.reshape(B, n_state), h0_im.reshape(B, n_state)]).astype(F32)
    ssm_w = u.shape[1]
    u_tm = u.reshape(B, L, ssm_w).transpose(1, 0, 2).reshape(T, ssm_w)
    s_tm, h_fin = _ssm(u_tm, h0, p['a'], *p['b'], *p['cr'], *p['ci'], p['d'], p['wg'], p['bg'],
                       streams=B, steps=min(SSM_STEPS, L))
    s = s_tm.reshape(L, B, ssm_w).transpose(1, 0, 2).reshape(T, ssm_w)
    x1, xn, scores = _outproj(x2d, o, s, p['w_out'], p['g2'], p['wq'], p['keys'], tm=min(OUT_ROWS, T))
    idx_t, gates = _topk(scores)
    y = _experts(idx_t.T, xn, x1, gates, p['uv'])
    G = n_state // SSM_N
    return (y.reshape(B, L, D), kf.reshape(B, L, ATT_HEADS, 2, ATT_HD), vf.reshape(B, L, ATT_HEADS, HEAD_W),
            h_fin[0].reshape(B, G, SSM_N), h_fin[1].reshape(B, G, SSM_N))


def kernel(x_prompt, x_sample, cache_k, cache_v, state_ssm_re, state_ssm_im, norm1_g, w_in, q_norm_g, k_norm_g, lambda_q1, lambda_k1, lambda_q2, lambda_k2, subln_g, ssm_a_re, ssm_a_im, ssm_log_dt, ssm_b_re, ssm_b_im, ssm_c_re, ssm_c_im, ssm_d, w_glu, b_glu, w_out, norm2_g, w_peer_q, peer_keys, peer_u, peer_v):
    depth = w_in.shape[0]
    Lp, Ls, past = x_prompt.shape[1], x_sample.shape[1], cache_k.shape[2]
    pos_p = jnp.arange(Lp, dtype=jnp.int32)
    pos_s = past + jnp.arange(Ls, dtype=jnp.int32)
    lane = jnp.arange(LANES)
    bd = (lane[:, None] // ATT_HD == lane[None, :] // ATT_HD).astype(BF16)
    yp, ys = x_prompt, x_sample
    outs = [[] for _ in range(8)]
    for layer in range(depth):
        lam_init = 0.8 - 0.6 * math.exp(-0.3 * layer)
        a, b, cr, ci = _ssm_params(ssm_a_re[layer], ssm_a_im[layer], ssm_log_dt[layer], ssm_b_re[layer],
                                   ssm_b_im[layer], ssm_c_re[layer], ssm_c_im[layer])
        slab_rows = peer_u.shape[-1] // LANES
        p = dict(
            g1=norm1_g[layer][None].astype(F32), w_in=w_in[layer].astype(BF16), bd=bd,
            qg=jnp.tile(q_norm_g[layer].astype(F32), LANES // ATT_HD)[None],
            kg=jnp.tile(k_norm_g[layer].astype(F32), LANES // ATT_HD)[None],
            lam=jnp.stack([lambda_q1[layer], lambda_k1[layer], lambda_q2[layer], lambda_k2[layer]]).astype(F32),
            subln=subln_g[layer][None].astype(F32),
            a=a, b=b, cr=cr, ci=ci, d=ssm_d[layer][None].astype(F32), wg=w_glu[layer].astype(BF16),
            bg=b_glu[layer][None].astype(F32), w_out=w_out[layer].astype(BF16),
            g2=norm2_g[layer][None].astype(F32), wq=w_peer_q[layer].astype(BF16),
            keys=peer_keys[layer].reshape(PEER_HEADS * 2, PEER_KEYS, PEER_HALF).astype(BF16),
            uv=jnp.stack([peer_u[layer].reshape(-1, slab_rows, LANES),
                          peer_v[layer].reshape(-1, slab_rows, LANES)], axis=1).astype(BF16))
        yp, kp, vp, hrp, hip = _layer(yp, pos_p, None, None, None, None, lam_init, p)
        ys, kk, vv, hrs, his = _layer(ys, pos_s, cache_k[layer], cache_v[layer], state_ssm_re[layer],
                                      state_ssm_im[layer], lam_init, p)
        for lst, val in zip(outs, (kp, vp, hrp, hip, kk, vv, hrs, his)):
            lst.append(val)
    return (yp, ys) + tuple(jnp.stack(l) for l in outs)
```

```python
import functools
import math

import jax
import jax.numpy as jnp
from jax import lax
from jax.experimental import pallas as pl
from jax.experimental.pallas import tpu as pltpu

F32 = jnp.float32
BF16 = jnp.bfloat16

CHUNK = 64
ATT_HEADS = 8
ATT_HD = 64
HEAD_W = 2 * ATT_HD
ATT_WIDTH = ATT_HEADS * HEAD_W
ROPE_DIM = ATT_HD // 4
ROPE_THETA = 500000.0
SSM_P = 16
SSM_N = 64
PEER_HEADS = 8
PEER_KEYS = 128
PEER_HALF = 128
PEER_TOPK = 16
EPS = 1e-6
NEG_INF = -1e30

LANES = 128
SUBLANES = 8
VMEM_LIMIT_BYTES = 56 * 1024 * 1024

INPROJ_ROWS = 256
ATTN_Q_ROWS = 256
SSM_STEPS = 32
SSM_GROUP_BLOCK = 16
SCAN_COLS = 512
OUT_ROWS = 256
TOPK_TOKENS = 128
EXPERT_TOKENS = 128
EXPERT_SLOTS = 8


def _cparams(sem):
    return pltpu.CompilerParams(dimension_semantics=sem, vmem_limit_bytes=VMEM_LIMIT_BYTES)


def _const_spec(shape):
    nd = len(shape)
    return pl.BlockSpec(shape, lambda *_: (0,) * nd, pipeline_mode=pl.Buffered(1))


def _split_bf16(a):
    hi = a.astype(BF16)
    lo = (a - hi.astype(F32)).astype(BF16)
    return hi, lo


def _dot(a, b):
    return jnp.dot(a, b, preferred_element_type=F32)


def _dot3(a_hi, a_lo, w_hi, w_lo):
    return _dot(a_hi, w_hi) + _dot(a_lo, w_hi) + _dot(a_hi, w_lo)


def _inproj_kernel(x_ref, g1_ref, w_ref, bd_ref, qg_ref, kg_ref, cos_ref, sa_ref, sb_ref,
                   qb_ref, kf_ref, kb_ref, vf_ref, vb_ref, u_ref):
    x = x_ref[...]
    ms = jnp.mean(x * x, axis=-1, keepdims=True)
    xn = (x * lax.rsqrt(ms + EPS) * g1_ref[...]).astype(BF16)
    proj = _dot(xn, w_ref[...])
    bd = bd_ref[...]
    cos, sa, sb = cos_ref[...], sa_ref[...], sb_ref[...]

    def head_norm_rope(t, g):
        hi, lo = _split_bf16(t * t)
        ssq = _dot(hi, bd) + _dot(lo, bd)
        n = t * lax.rsqrt(ssq * (1.0 / ATT_HD) + EPS) * g
        return n * cos + pltpu.roll(n, LANES - ROPE_DIM // 2, 1) * sa + pltpu.roll(n, ROPE_DIM // 2, 1) * sb

    for h in range(ATT_HEADS):
        c0 = h * HEAD_W
        q = head_norm_rope(proj[:, c0:c0 + HEAD_W], qg_ref[...])
        qb_ref[:, c0:c0 + HEAD_W] = (q * (ATT_HD ** -0.5)).astype(BF16)
        k = head_norm_rope(proj[:, ATT_WIDTH + c0:ATT_WIDTH + c0 + HEAD_W], kg_ref[...])
        kf_ref[:, c0:c0 + HEAD_W] = k
        kb_ref[:, c0:c0 + HEAD_W] = k.astype(BF16)
    v = proj[:, 2 * ATT_WIDTH:3 * ATT_WIDTH]
    vf_ref[...] = v
    vb_ref[...] = v.astype(BF16)
    u_ref[...] = proj[:, 3 * ATT_WIDTH:]


def _inproj(x2d, g1, w_bf, bd, qg, kg, cos, sa, sb, *, tm, pos_blocks):
    T, D = x2d.shape
    n_in = w_bf.shape[1]
    ssm_w = n_in - 3 * ATT_WIDTH
    row = lambda w: pl.BlockSpec((tm, w), lambda i: (i, 0))
    pos = pl.BlockSpec((tm, LANES), lambda i: (i % pos_blocks, 0))
    outs = [jax.ShapeDtypeStruct((T, ATT_WIDTH), BF16), jax.ShapeDtypeStruct((T, ATT_WIDTH), F32),
            jax.ShapeDtypeStruct((T, ATT_WIDTH), BF16), jax.ShapeDtypeStruct((T, ATT_WIDTH), F32),
            jax.ShapeDtypeStruct((T, ATT_WIDTH), BF16), jax.ShapeDtypeStruct((T, ssm_w), F32)]
    return pl.pallas_call(
        _inproj_kernel,
        grid=(T // tm,),
        in_specs=[row(D), _const_spec((1, D)), _const_spec((D, n_in)), _const_spec((LANES, LANES)),
                  _const_spec((1, LANES)), _const_spec((1, LANES)), pos, pos, pos],
        out_specs=[row(ATT_WIDTH)] * 5 + [row(ssm_w)],
        out_shape=outs,
        compiler_params=_cparams(("parallel",)),
        name="inproj",
    )(x2d, g1, w_bf, bd, qg, kg, cos, sa, sb)


def _diff_lambda(lam_ref, lam_init):
    lv = lam_ref[...]
    a = jnp.sum(lv[0:1] * lv[1:2], axis=-1, keepdims=True)
    b = jnp.sum(lv[2:3] * lv[3:4], axis=-1, keepdims=True)
    return jnp.exp(a) - jnp.exp(b) + lam_init


def _stack_maps(q):
    lane = lax.broadcasted_iota(jnp.int32, q.shape, 1)
    zero = jnp.zeros_like(q)
    return jnp.concatenate([jnp.where(lane < ATT_HD, q, zero), jnp.where(lane >= ATT_HD, q, zero)], axis=0)


def _attn_finish(acc, l, tq, lam, g, lam_init):
    o = acc[:tq] / l[:tq] - lam * (acc[tq:] / l[tq:])
    ms = jnp.mean(o * o, axis=-1, keepdims=True)
    return (o * lax.rsqrt(ms + EPS) * g * (1.0 - lam_init)).astype(BF16)


def _qk(qq, kb):
    return lax.dot_general(qq, kb, (((1,), (1,)), ((), ())), preferred_element_type=F32)


def _attn_prompt_kernel(lam_ref, g_ref, q_ref, k_ref, v_ref, o_ref, *, tq, lam_init):
    i = pl.program_id(2)
    qq = _stack_maps(q_ref[...])

    def update(carry, s, vb):
        m, l, acc = carry
        m_new = jnp.maximum(m, jnp.max(s, axis=1, keepdims=True))
        alpha = jnp.exp(m - m_new)
        p = jnp.exp(s - m_new)
        l = alpha * l + jnp.sum(p, axis=1, keepdims=True)
        acc = alpha * acc + _dot(p.astype(BF16), vb)
        return m_new, l, acc

    def full_block(j, carry):
        rows = pl.ds(pl.multiple_of(j * tq, tq), tq)
        return update(carry, _qk(qq, k_ref[rows, :]), v_ref[rows, :])

    carry = (jnp.full((2 * tq, 1), NEG_INF, F32), jnp.zeros((2 * tq, 1), F32),
             jnp.zeros((2 * tq, HEAD_W), F32))
    carry = lax.fori_loop(0, i, full_block, carry)
    rows = pl.ds(pl.multiple_of(i * tq, tq), tq)
    s = _qk(qq, k_ref[rows, :])
    r = lax.broadcasted_iota(jnp.int32, s.shape, 0) % tq
    c = lax.broadcasted_iota(jnp.int32, s.shape, 1)
    s = jnp.where(c // CHUNK <= r // CHUNK, s, NEG_INF)
    _, l, acc = update(carry, s, v_ref[rows, :])
    o_ref[...] = _attn_finish(acc, l, tq, _diff_lambda(lam_ref, lam_init), g_ref[...], lam_init)


def _attn_prompt(lam_vecs, subln_g, qb, kb, vb, *, batch, seq, lam_init):
    tq = ATTN_Q_ROWS
    nq = seq // tq
    T = batch * seq
    qspec = pl.BlockSpec((tq, HEAD_W), lambda b, h, i: (b * nq + i, h))
    kvspec = pl.BlockSpec((seq, HEAD_W), lambda b, h, i: (b, h))
    return pl.pallas_call(
        functools.partial(_attn_prompt_kernel, tq=tq, lam_init=lam_init),
        grid=(batch, ATT_HEADS, nq),
        in_specs=[_const_spec((4, ATT_HD)), _const_spec((1, HEAD_W)), qspec, kvspec, kvspec],
        out_specs=qspec,
        out_shape=jax.ShapeDtypeStruct((T, ATT_WIDTH), BF16),
        compiler_params=_cparams(("parallel", "parallel", "arbitrary")),
        name="attn_prompt",
    )(lam_vecs, subln_g, qb, kb, vb)


def _attn_sample_kernel(lam_ref, g_ref, q_ref, kp_ref, vp_ref, kn_ref, vn_ref, o_ref, *, tq, lam_init):
    qq = _stack_maps(q_ref[...])
    s_past = _qk(qq, kp_ref[...].astype(BF16))
    s_new = _qk(qq, kn_ref[...])
    m = jnp.maximum(jnp.max(s_past, axis=1, keepdims=True), jnp.max(s_new, axis=1, keepdims=True))
    p_past = jnp.exp(s_past - m)
    p_new = jnp.exp(s_new - m)
    l = jnp.sum(p_past, axis=1, keepdims=True) + jnp.sum(p_new, axis=1, keepdims=True)
    acc = _dot(p_past.astype(BF16), vp_ref[...].astype(BF16)) + _dot(p_new.astype(BF16), vn_ref[...])
    o_ref[...] = _attn_finish(acc, l, tq, _diff_lambda(lam_ref, lam_init), g_ref[...], lam_init)


def _attn_sample(lam_vecs, subln_g, qb, k_past, v_past, kb, vb, *, batch, seq, past, lam_init):
    new = pl.BlockSpec((seq, HEAD_W), lambda b, h: (b, h))
    old = pl.BlockSpec((past, HEAD_W), lambda b, h: (b, h))
    return pl.pallas_call(
        functools.partial(_attn_sample_kernel, tq=seq, lam_init=lam_init),
        grid=(batch, ATT_HEADS),
        in_specs=[_const_spec((4, ATT_HD)), _const_spec((1, HEAD_W)), new, old, old, new, new],
        out_specs=new,
        out_shape=jax.ShapeDtypeStruct((batch * seq, ATT_WIDTH), BF16),
        compiler_params=_cparams(("parallel", "parallel")),
        name="attn_sample",
    )(lam_vecs, subln_g, qb, k_past, v_past, kb, vb)


def _ssm_kernel(u_ref, h0_ref, a_ref, bh_ref, bl_ref, crh_ref, crl_ref, cih_ref, cil_ref,
                d_ref, wg_ref, bg_ref, s_ref, hout_ref, bu_ref, h_ref, *, steps, streams):
    n_state = a_ref.shape[1]
    n_blocks = bh_ref.shape[0]
    in_w = bh_ref.shape[1]
    st_w = n_state // n_blocks

    @pl.when(pl.program_id(0) == 0)
    def _():
        h_ref[...] = h0_ref[...]

    u = u_ref[...]
    u_hi, u_lo = _split_bf16(u)
    for gb in range(n_blocks):
        cols = slice(gb * in_w, (gb + 1) * in_w)
        bu = _dot3(u_hi[:, cols], u_lo[:, cols], bh_ref[gb], bl_ref[gb])
        bu_ref[:, gb * st_w:(gb + 1) * st_w] = bu[:, :st_w]
        bu_ref[:, n_state + gb * st_w:n_state + (gb + 1) * st_w] = bu[:, st_w:]

    for c in range(n_state // SCAN_COLS):
        re = slice(c * SCAN_COLS, (c + 1) * SCAN_COLS)
        im = slice(n_state + c * SCAN_COLS, n_state + (c + 1) * SCAN_COLS)
        ar = jnp.broadcast_to(a_ref[0:1, re], (streams, SCAN_COLS))
        ai = jnp.broadcast_to(a_ref[1:2, re], (streams, SCAN_COLS))

        def step(t, carry):
            hr, hi = carry
            rows = pl.ds(pl.multiple_of(t * streams, streams), streams)
            nr = ar * hr - ai * hi + bu_ref[rows, re]
            ni = ar * hi + ai * hr + bu_ref[rows, im]
            bu_ref[rows, re] = nr
            bu_ref[rows, im] = ni
            return nr, ni

        hr, hi = lax.fori_loop(0, steps, step, (h_ref[0, :, re], h_ref[1, :, re]), unroll=4)
        h_ref[0, :, re] = hr
        h_ref[1, :, re] = hi
    hout_ref[...] = h_ref[...]

    ys = []
    for gb in range(n_blocks):
        hr_hi, hr_lo = _split_bf16(bu_ref[:, gb * st_w:(gb + 1) * st_w])
        hi_hi, hi_lo = _split_bf16(bu_ref[:, n_state + gb * st_w:n_state + (gb + 1) * st_w])
        ys.append(_dot3(hr_hi, hr_lo, crh_ref[gb], crl_ref[gb]) + _dot3(hi_hi, hi_lo, cih_ref[gb], cil_ref[gb]))
    y = jax.nn.gelu(jnp.concatenate(ys, axis=1) + d_ref[...] * u)
    z = _dot(y.astype(BF16), wg_ref[...]) + bg_ref[...]
    s_ref[...] = (y * jax.nn.sigmoid(z)).astype(BF16)


def _ssm(u_tm, h0, a, bh, bl, crh, crl, cih, cil, d, wg, bg, *, streams, steps):
    rows_total, width = u_tm.shape
    rows = steps * streams
    n_state = a.shape[1]
    row = pl.BlockSpec((rows, width), lambda i: (i, 0))
    consts = [h0, a, bh, bl, crh, crl, cih, cil, d, wg, bg]
    return pl.pallas_call(
        functools.partial(_ssm_kernel, steps=steps, streams=streams),
        grid=(rows_total // rows,),
        in_specs=[row] + [_const_spec(c.shape) for c in consts],
        out_specs=[row, _const_spec(h0.shape)],
        out_shape=[jax.ShapeDtypeStruct((rows_total, width), BF16), jax.ShapeDtypeStruct(h0.shape, F32)],
        scratch_shapes=[pltpu.VMEM((rows, 2 * n_state), F32), pltpu.VMEM(h0.shape, F32)],
        compiler_params=_cparams(("arbitrary",)),
        name="ssm",
    )(u_tm, *consts)


def _ssm_params(a_re, a_im, log_dt, b_re, b_im, c_re, c_im):
    G, N, P = b_re.shape
    gb_n = SSM_GROUP_BLOCK
    nb = G // gb_n
    dt = jnp.exp(log_dt.astype(F32))[:, None]
    ar, ai = a_re.astype(F32), a_im.astype(F32)
    mag = jnp.exp(dt * ar)
    abar_r = mag * jnp.cos(dt * ai)
    abar_i = mag * jnp.sin(dt * ai)
    den = ar * ar + ai * ai
    nr, ni = abar_r - 1.0, abar_i
    coef_r = (nr * ar + ni * ai) / den
    coef_i = (ni * ar - nr * ai) / den
    br, bi = b_re.astype(F32), b_im.astype(F32)
    bbar_r = coef_r[..., None] * br - coef_i[..., None] * bi
    bbar_i = coef_r[..., None] * bi + coef_i[..., None] * br
    eye = jnp.eye(gb_n, dtype=F32)

    def in_block(bb):
        w = bb.reshape(nb, gb_n, N, P).transpose(0, 1, 3, 2)
        return jnp.einsum('bgpn,gh->bgphn', w, eye).reshape(nb, gb_n * P, gb_n * N)

    def out_block(cc):
        w = cc.reshape(nb, gb_n, P, N).transpose(0, 1, 3, 2)
        return jnp.einsum('bgnp,gh->bgnhp', w, eye).reshape(nb, gb_n * N, gb_n * P)

    b_bd = jnp.concatenate([in_block(bbar_r), in_block(bbar_i)], axis=2)
    a = jnp.stack([abar_r.reshape(-1), abar_i.reshape(-1)])
    return a, _split_bf16(b_bd), _split_bf16(out_block(c_re.astype(F32))), _split_bf16(-out_block(c_im.astype(F32)))


def _outproj_kernel(x_ref, o_ref, s_ref, wo_ref, g2_ref, wq_ref, keys_ref, x1_ref, xn_ref, sc_ref):
    aw = o_ref.shape[1]
    x1 = x_ref[...] + _dot(o_ref[...], wo_ref[:aw, :]) + _dot(s_ref[...], wo_ref[aw:, :])
    x1_ref[...] = x1
    ms = jnp.mean(x1 * x1, axis=-1, keepdims=True)
    xn = x1 * lax.rsqrt(ms + EPS) * g2_ref[...]
    xn_ref[...] = xn
    q = _dot(xn.astype(BF16), wq_ref[...])
    for j in range(keys_ref.shape[0]):
        qj = q[:, j * PEER_HALF:(j + 1) * PEER_HALF].astype(BF16)
        sc_ref[j] = _qk(keys_ref[j], qj)


def _outproj(x2d, o_b, s_b, wo_bf, g2, wq_bf, keys_bf, *, tm):
    T, D = x2d.shape
    nset = keys_bf.shape[0]
    row = lambda w: pl.BlockSpec((tm, w), lambda i: (i, 0))
    return pl.pallas_call(
        _outproj_kernel,
        grid=(T // tm,),
        in_specs=[row(D), row(o_b.shape[1]), row(s_b.shape[1]), _const_spec(wo_bf.shape), _const_spec((1, D)),
                  _const_spec(wq_bf.shape), _const_spec(keys_bf.shape)],
        out_specs=[row(D), row(D), pl.BlockSpec((nset, PEER_KEYS, tm), lambda i: (0, 0, i))],
        out_shape=[jax.ShapeDtypeStruct((T, D), F32), jax.ShapeDtypeStruct((T, D), F32),
                   jax.ShapeDtypeStruct((nset, PEER_KEYS, T), F32)],
        compiler_params=_cparams(("parallel",)),
        name="outproj",
    )(x2d, o_b, s_b, wo_bf, g2, wq_bf, keys_bf)


def _extract_topk(vals, pos, payload, k):
    n, tt = vals.shape
    slot = lax.broadcasted_iota(jnp.int32, (k, tt), 0)
    out_v = jnp.zeros((k, tt), F32)
    out_p = jnp.zeros((k, tt), F32)
    for r in range(k):
        m = jnp.max(vals, axis=0, keepdims=True)
        first = jnp.min(jnp.where(vals == m, pos, jnp.inf), axis=0, keepdims=True)
        hit = pos == first
        picked = first if payload is None else jnp.max(jnp.where(hit, payload, -1.0), axis=0, keepdims=True)
        out_v = jnp.where(slot == r, m, out_v)
        out_p = jnp.where(slot == r, picked, out_p)
        vals = jnp.where(hit, -jnp.inf, vals)
    return out_v, out_p


def _staircase_candidates(a, ia, b, ib, k):
    rows = SUBLANES
    row = lax.broadcasted_iota(jnp.int32, (rows, a.shape[1]), 0)
    rowf = row.astype(F32)
    vals, pos, eid = [], [], []

    def add(v, p, e, valid=None):
        vals.append(v if valid is None else jnp.where(valid, v, -jnp.inf))
        pos.append(p)
        eid.append(e)

    i = 0
    while k // (i + 1) >= rows:
        for j0 in range(0, k // (i + 1), rows):
            add(a[i:i + 1] + b[j0:j0 + rows], float(i * k + j0) + rowf,
                ia[i:i + 1] * float(PEER_KEYS) + ib[j0:j0 + rows])
        i += 1
    while k // (i + 1) > 1:
        add(a[i:i + 1] + b[:rows], float(i * k) + rowf, ia[i:i + 1] * float(PEER_KEYS) + ib[:rows],
            valid=row < k // (i + 1))
        i += 1
    for i0 in range(i, k, rows):
        add(a[i0:i0 + rows] + b[0:1], (float(i0) + rowf) * float(k), ia[i0:i0 + rows] * float(PEER_KEYS) + ib[0:1])
    return tuple(jnp.concatenate(x, axis=0) for x in (vals, pos, eid))


def _topk_kernel(sc_ref, idx_ref, gate_ref):
    k = PEER_TOPK
    tt = sc_ref.shape[2]
    key_id = lax.broadcasted_iota(jnp.int32, (PEER_KEYS, tt), 0).astype(F32)

    def head(h, _):
        v0, i0 = _extract_topk(sc_ref[2 * h], key_id, None, k)
        v1, i1 = _extract_topk(sc_ref[2 * h + 1], key_id, None, k)
        fv, fe = _extract_topk(*_staircase_candidates(v0, i0, v1, i1, k), k)
        e = jnp.exp(fv - fv[0:1])
        rows = pl.ds(pl.multiple_of(h * k, k), k)
        gate_ref[rows, :] = e / jnp.sum(e, axis=0, keepdims=True)
        idx_ref[rows, :] = fe.astype(jnp.int32)
        return 0

    lax.fori_loop(0, PEER_HEADS, head, 0, unroll=2)


def _topk(scores):
    nset, nkeys, T = scores.shape
    tt = TOPK_TOKENS
    out = pl.BlockSpec((PEER_HEADS * PEER_TOPK, tt), lambda i: (0, i))
    return pl.pallas_call(
        _topk_kernel,
        grid=(T // tt,),
        in_specs=[pl.BlockSpec((nset, nkeys, tt), lambda i: (0, 0, i))],
        out_specs=[out, out],
        out_shape=[jax.ShapeDtypeStruct((PEER_HEADS * PEER_TOPK, T), jnp.int32),
                   jax.ShapeDtypeStruct((PEER_HEADS * PEER_TOPK, T), F32)],
        compiler_params=_cparams(("parallel",)),
        name="peer_topk",
    )(scores)


def _fold_pairs(vs, shift, keep):
    return [jnp.where(keep, a, b) + pltpu.roll(jnp.where(keep, b, a), shift, 0) for a, b in zip(vs[0::2], vs[1::2])]


def _expert_kernel(idx_ref, xn_ref, x1_ref, gate_ref, uv_hbm, y_ref, buf, sem, cb_ref, *, tokens, slots):
    n_sel = buf.shape[1]
    half = SUBLANES
    lane = lax.broadcasted_iota(jnp.int32, (n_sel, tokens), 1)
    sub = lax.broadcasted_iota(jnp.int32, (SUBLANES, LANES), 0)
    keep1, keep2, keep4 = sub % 2 == 0, sub % 4 < 2, sub < 4

    def issue(t, ks):
        slot = t % slots
        for k in ks:
            pltpu.make_async_copy(uv_hbm.at[idx_ref[t, k]], buf.at[slot, k], sem.at[slot]).start(priority=k % 2)

    def wait(t):
        slot = t % slots
        pltpu.make_async_copy(uv_hbm.at[pl.ds(0, n_sel)], buf.at[slot], sem.at[slot]).wait()

    n_groups = n_sel // SUBLANES
    per_chunk = n_sel // (2 * n_groups)

    def compute(t, t_next):
        slot = t % slots
        row = pl.ds(t, 1)
        chunk = lambda r: slice(r * LANES, (r + 1) * LANES)
        xrow = xn_ref[row, :]
        xa = jnp.concatenate([xrow[:, chunk(r)] for r in range(half)], axis=0)
        xb = jnp.concatenate([xrow[:, chunk(half + r)] for r in range(half)], axis=0)
        folded = []
        for g in range(n_groups):
            if t_next is not None:
                issue(t_next, range(g * per_chunk, (g + 1) * per_chunk))
            ps = []
            for k in range(g * SUBLANES, (g + 1) * SUBLANES):
                w = buf[slot, k, 0].astype(F32)
                ps.append(w[:half] * xa + w[half:] * xb)
            ps = _fold_pairs(_fold_pairs(_fold_pairs(ps, 1, keep1), 2, keep2), 4, keep4)
            folded.append(ps[0])
        act = jnp.sum(jnp.concatenate(folded, axis=0), axis=1, keepdims=True)
        gate = jnp.sum(jnp.where(lane == t, gate_ref[...], 0.0), axis=1, keepdims=True)
        cb_ref[...] = jnp.broadcast_to(gate * jax.nn.gelu(act), cb_ref.shape)
        n_acc = 4
        acc_a = [jnp.zeros((half, LANES), F32)] * n_acc
        acc_b = [jnp.zeros((half, LANES), F32)] * n_acc
        for g in range(n_groups):
            if t_next is not None:
                issue(t_next, range((n_groups + g) * per_chunk, (n_groups + g + 1) * per_chunk))
            for k in range(g * SUBLANES, (g + 1) * SUBLANES):
                c = jnp.broadcast_to(cb_ref[pl.ds(k, 1), :], (half, LANES))
                w = buf[slot, k, 1].astype(F32)
                acc_a[k % n_acc] = acc_a[k % n_acc] + c * w[:half]
                acc_b[k % n_acc] = acc_b[k % n_acc] + c * w[half:]
        out_a = (acc_a[0] + acc_a[1]) + (acc_a[2] + acc_a[3])
        out_b = (acc_b[0] + acc_b[1]) + (acc_b[2] + acc_b[3])
        out_row = jnp.concatenate([o[r:r + 1] for o in (out_a, out_b) for r in range(half)], axis=1)
        y_ref[row, :] = x1_ref[row, :] + out_row

    ahead = slots - 1

    def fill(t, c):
        issue(t, range(n_sel))
        return c

    def steady(t, c):
        wait(t)
        compute(t, t + ahead)
        return c

    def drain(t, c):
        wait(t)
        compute(t, None)
        return c

    lax.fori_loop(0, ahead, fill, 0)
    lax.fori_loop(0, tokens - ahead, steady, 0)
    lax.fori_loop(tokens - ahead, tokens, drain, 0)


def _experts(idx_tok, xn, x1, gates, uv):
    T, D = xn.shape
    n_sel = idx_tok.shape[1]
    tt = EXPERT_TOKENS
    row = pl.BlockSpec((tt, D), lambda i: (i, 0))
    return pl.pallas_call(
        functools.partial(_expert_kernel, tokens=tt, slots=EXPERT_SLOTS),
        grid=(T // tt,),
        in_specs=[pl.BlockSpec((tt, n_sel), lambda i: (i, 0), memory_space=pltpu.SMEM), row, row,
                  pl.BlockSpec((n_sel, tt), lambda i: (0, i)), pl.BlockSpec(memory_space=pl.ANY)],
        out_specs=row,
        out_shape=jax.ShapeDtypeStruct((T, D), F32),
        scratch_shapes=[pltpu.VMEM((EXPERT_SLOTS, n_sel) + uv.shape[1:], uv.dtype),
                        pltpu.SemaphoreType.DMA((EXPERT_SLOTS,)), pltpu.VMEM((n_sel, LANES), F32)],
        compiler_params=_cparams(("arbitrary",)),
        name="peer_experts",
    )(idx_tok, xn, x1, gates, uv)


def _rope_tables(pos):
    half = ROPE_DIM // 2
    inv_freq = ROPE_THETA ** (-jnp.arange(half, dtype=F32) * (2.0 / ROPE_DIM))
    ang = pos.astype(F32)[:, None] * inv_freq[None, :]
    cos, sin = jnp.cos(ang), jnp.sin(ang)
    n = pos.shape[0]
    pad = jnp.zeros((n, ATT_HD - ROPE_DIM), F32)
    zero = jnp.zeros((n, half), F32)
    c = jnp.concatenate([cos, cos, pad + 1.0], axis=1)
    sa = jnp.concatenate([-sin, zero, pad], axis=1)
    sb = jnp.concatenate([zero, sin, pad], axis=1)
    return tuple(jnp.tile(t, (1, LANES // ATT_HD)) for t in (c, sa, sb))


def _layer(x, pos, past_k, past_v, h0_re, h0_im, lam_init, p):
    B, L, D = x.shape
    T = B * L
    x2d = x.reshape(T, D)
    tm = min(INPROJ_ROWS, L)
    cos, sa, sb = _rope_tables(pos)
    qb, kf, kb, vf, vb, u = _inproj(x2d, p['g1'], p['w_in'], p['bd'], p['qg'], p['kg'], cos, sa, sb,
                                   tm=tm, pos_blocks=L // tm)
    if past_k is None:
        o = _attn_prompt(p['lam'], p['subln'], qb, kb, vb, batch=B, seq=L, lam_init=lam_init)
    else:
        past = past_k.shape[1]
        o = _attn_sample(p['lam'], p['subln'], qb, past_k.reshape(B * past, ATT_WIDTH),
                         past_v.reshape(B * past, ATT_WIDTH), kb, vb, batch=B, seq=L, past=past,
                         lam_init=lam_init)
    n_state = p['a'].shape[1]
    if h0_re is None:
        h0 = jnp.zeros((2, B, n_state), F32)
    else:
        h0 = jnp.stack([h0_re.reshape(B, n_state), h0_im.reshape(B, n_state)]).astype(F32)
    ssm_w = u.shape[1]
    u_tm = u.reshape(B, L, ssm_w).transpose(1, 0, 2).reshape(T, ssm_w)
    s_tm, h_fin = _ssm(u_tm, h0, p['a'], *p['b'], *p['cr'], *p['ci'], p['d'], p['wg'], p['bg'],
                       streams=B, steps=min(SSM_STEPS, L))
    s = s_tm.reshape(L, B, ssm_w).transpose(1, 0, 2).reshape(T, ssm_w)
    x1, xn, scores = _outproj(x2d, o, s, p['w_out'], p['g2'], p['wq'], p['keys'], tm=min(OUT_ROWS, T))
    idx_t, gates = _topk(scores)
    y = _experts(idx_t.T, xn, x1, gates, p['uv'])
    G = n_state // SSM_N
    return (y.reshape(B, L, D), kf.reshape(B, L, ATT_HEADS, 2, ATT_HD), vf.reshape(B, L, ATT_HEADS, HEAD_W),
            h_fin[0].reshape(B, G, SSM_N), h_fin[1].reshape(B, G, SSM_N))


def kernel(x_prompt, x_sample, cache_k, cache_v, state_ssm_re, state_ssm_im, norm1_g, w_in, q_norm_g, k_norm_g, lambda_q1, lambda_k1, lambda_q2, lambda_k2, subln_g, ssm_a_re, ssm_a_im, ssm_log_dt, ssm_b_re, ssm_b_im, ssm_c_re, ssm_c_im, ssm_d, w_glu, b_glu, w_out, norm2_g, w_peer_q, peer_keys, peer_u, peer_v):
    depth = w_in.shape[0]
    Lp, Ls, past = x_prompt.shape[1], x_sample.shape[1], cache_k.shape[2]
    pos_p = jnp.arange(Lp, dtype=jnp.int32)
    pos_s = past + jnp.arange(Ls, dtype=jnp.int32)
    lane = jnp.arange(LANES)
    bd = (lane[:, None] // ATT_HD == lane[None, :] // ATT_HD).astype(BF16)
    yp, ys = x_prompt, x_sample
    outs = [[] for _ in range(8)]
    for layer in range(depth):
        lam_init = 0.8 - 0.6 * math.exp(-0.3 * layer)
        a, b, cr, ci = _ssm_params(ssm_a_re[layer], ssm_a_im[layer], ssm_log_dt[layer], ssm_b_re[layer],
                                   ssm_b_im[layer], ssm_c_re[layer], ssm_c_im[layer])
        slab_rows = peer_u.shape[-1] // LANES
        p = dict(
            g1=norm1_g[layer][None].astype(F32), w_in=w_in[layer].astype(BF16), bd=bd,
            qg=jnp.tile(q_norm_g[layer].astype(F32), LANES // ATT_HD)[None],
            kg=jnp.tile(k_norm_g[layer].astype(F32), LANES // ATT_HD)[None],
            lam=jnp.stack([lambda_q1[layer], lambda_k1[layer], lambda_q2[layer], lambda_k2[layer]]).astype(F32),
            subln=subln_g[layer][None].astype(F32),
            a=a, b=b, cr=cr, ci=ci, d=ssm_d[layer][None].astype(F32), wg=w_glu[layer].astype(BF16),
            bg=b_glu[layer][None].astype(F32), w_out=w_out[layer].astype(BF16),
            g2=norm2_g[layer][None].astype(F32), wq=w_peer_q[layer].astype(BF16),
            keys=peer_keys[layer].reshape(PEER_HEADS * 2, PEER_KEYS, PEER_HALF).astype(BF16),
            uv=jnp.stack([peer_u[layer].astype(BF16).reshape(-1, slab_rows, LANES),
                          peer_v[layer].astype(BF16).reshape(-1, slab_rows, LANES)], axis=1))
        yp, kp, vp, hrp, hip = _layer(yp, pos_p, None, None, None, None, lam_init, p)
        ys, kk, vv, hrs, his = _layer(ys, pos_s, cache_k[layer], cache_v[layer], state_ssm_re[layer],
                                      state_ssm_im[layer], lam_init, p)
        for lst, val in zip(outs, (kp, vp, hrp, hip, kk, vv, hrs, his)):
            lst.append(val)
    return (yp, ys) + tuple(jnp.stack(l) for l in outs)
```

```python
import functools
import math

import jax
import jax.numpy as jnp
from jax import lax
from jax.experimental import pallas as pl
from jax.experimental.pallas import tpu as pltpu

F32 = jnp.float32
BF16 = jnp.bfloat16

CHUNK = 64
ATT_HEADS = 8
ATT_HD = 64
HEAD_W = 2 * ATT_HD
ATT_WIDTH = ATT_HEADS * HEAD_W
ROPE_DIM = ATT_HD // 4
ROPE_THETA = 500000.0
SSM_P = 16
SSM_N = 64
PEER_HEADS = 8
PEER_KEYS = 128
PEER_HALF = 128
PEER_TOPK = 16
EPS = 1e-6
NEG_INF = -1e30

LANES = 128
SUBLANES = 8
VMEM_LIMIT_BYTES = 56 * 1024 * 1024

INPROJ_ROWS = 256
ATTN_Q_ROWS = 256
SSM_STEPS = 32
SSM_GROUP_BLOCK = 16
SCAN_COLS = 512
OUT_ROWS = 256
TOPK_TOKENS = 128
EXPERT_TOKENS = 128
EXPERT_SLOTS = 8


def _cparams(sem):
    return pltpu.CompilerParams(dimension_semantics=sem, vmem_limit_bytes=VMEM_LIMIT_BYTES)


def _const_spec(shape):
    nd = len(shape)
    return pl.BlockSpec(shape, lambda *_: (0,) * nd, pipeline_mode=pl.Buffered(1))


def _split_bf16(a):
    hi = a.astype(BF16)
    lo = (a - hi.astype(F32)).astype(BF16)
    return hi, lo


def _dot(a, b):
    return jnp.dot(a, b, preferred_element_type=F32)


def _inproj_kernel(x_ref, g1_ref, w_ref, bd_ref, qg_ref, kg_ref, cos_ref, sa_ref, sb_ref,
                   qb_ref, kf_ref, kb_ref, vf_ref, vb_ref, u_ref):
    x = x_ref[...]
    ms = jnp.mean(x * x, axis=-1, keepdims=True)
    xn = (x * lax.rsqrt(ms + EPS) * g1_ref[...]).astype(BF16)
    proj = _dot(xn, w_ref[...])
    bd = bd_ref[...]
    cos, sa, sb = cos_ref[...], sa_ref[...], sb_ref[...]

    def head_norm_rope(t, g):
        hi, lo = _split_bf16(t * t)
        ssq = _dot(hi, bd) + _dot(lo, bd)
        n = t * lax.rsqrt(ssq * (1.0 / ATT_HD) + EPS) * g
        return n * cos + pltpu.roll(n, LANES - ROPE_DIM // 2, 1) * sa + pltpu.roll(n, ROPE_DIM // 2, 1) * sb

    for h in range(ATT_HEADS):
        c0 = h * HEAD_W
        q = head_norm_rope(proj[:, c0:c0 + HEAD_W], qg_ref[...])
        qb_ref[:, c0:c0 + HEAD_W] = (q * (ATT_HD ** -0.5)).astype(BF16)
        k = head_norm_rope(proj[:, ATT_WIDTH + c0:ATT_WIDTH + c0 + HEAD_W], kg_ref[...])
        kf_ref[:, c0:c0 + HEAD_W] = k
        kb_ref[:, c0:c0 + HEAD_W] = k.astype(BF16)
    v = proj[:, 2 * ATT_WIDTH:3 * ATT_WIDTH]
    vf_ref[...] = v
    vb_ref[...] = v.astype(BF16)
    u_ref[...] = proj[:, 3 * ATT_WIDTH:]


def _inproj(x2d, g1, w_bf, bd, qg, kg, cos, sa, sb, *, tm, pos_blocks):
    T, D = x2d.shape
    n_in = w_bf.shape[1]
    ssm_w = n_in - 3 * ATT_WIDTH
    row = lambda w: pl.BlockSpec((tm, w), lambda i: (i, 0))
    pos = pl.BlockSpec((tm, LANES), lambda i: (i % pos_blocks, 0))
    outs = [jax.ShapeDtypeStruct((T, ATT_WIDTH), BF16), jax.ShapeDtypeStruct((T, ATT_WIDTH), F32),
            jax.ShapeDtypeStruct((T, ATT_WIDTH), BF16), jax.ShapeDtypeStruct((T, ATT_WIDTH), F32),
            jax.ShapeDtypeStruct((T, ATT_WIDTH), BF16), jax.ShapeDtypeStruct((T, ssm_w), F32)]
    return pl.pallas_call(
        _inproj_kernel,
        grid=(T // tm,),
        in_specs=[row(D), _const_spec((1, D)), _const_spec((D, n_in)), _const_spec((LANES, LANES)),
                  _const_spec((1, LANES)), _const_spec((1, LANES)), pos, pos, pos],
        out_specs=[row(ATT_WIDTH)] * 5 + [row(ssm_w)],
        out_shape=outs,
        compiler_params=_cparams(("parallel",)),
        name="inproj",
    )(x2d, g1, w_bf, bd, qg, kg, cos, sa, sb)


def _diff_lambda(lam_ref, lam_init):
    lv = lam_ref[...]
    a = jnp.sum(lv[0:1] * lv[1:2], axis=-1, keepdims=True)
    b = jnp.sum(lv[2:3] * lv[3:4], axis=-1, keepdims=True)
    return jnp.exp(a) - jnp.exp(b) + lam_init


def _stack_maps(q):
    lane = lax.broadcasted_iota(jnp.int32, q.shape, 1)
    zero = jnp.zeros_like(q)
    return jnp.concatenate([jnp.where(lane < ATT_HD, q, zero), jnp.where(lane >= ATT_HD, q, zero)], axis=0)


def _attn_finish(acc, l, tq, lam, g, lam_init):
    o = acc[:tq] / l[:tq] - lam * (acc[tq:] / l[tq:])
    ms = jnp.mean(o * o, axis=-1, keepdims=True)
    return (o * lax.rsqrt(ms + EPS) * g * (1.0 - lam_init)).astype(BF16)


def _qk(qq, kb):
    return lax.dot_general(qq, kb, (((1,), (1,)), ((), ())), preferred_element_type=F32)


def _attn_prompt_kernel(lam_ref, g_ref, q_ref, k_ref, v_ref, o_ref, *, tq, lam_init):
    i = pl.program_id(2)
    qq = _stack_maps(q_ref[...])

    def update(carry, s, vb):
        m, l, acc = carry
        m_new = jnp.maximum(m, jnp.max(s, axis=1, keepdims=True))
        alpha = jnp.exp(m - m_new)
        p = jnp.exp(s - m_new)
        l = alpha * l + jnp.sum(p, axis=1, keepdims=True)
        acc = alpha * acc + _dot(p.astype(BF16), vb)
        return m_new, l, acc

    def full_block(j, carry):
        rows = pl.ds(pl.multiple_of(j * tq, tq), tq)
        return update(carry, _qk(qq, k_ref[rows, :]), v_ref[rows, :])

    carry = (jnp.full((2 * tq, 1), NEG_INF, F32), jnp.zeros((2 * tq, 1), F32),
             jnp.zeros((2 * tq, HEAD_W), F32))
    carry = lax.fori_loop(0, i, full_block, carry)
    rows = pl.ds(pl.multiple_of(i * tq, tq), tq)
    s = _qk(qq, k_ref[rows, :])
    r = lax.broadcasted_iota(jnp.int32, s.shape, 0) % tq
    c = lax.broadcasted_iota(jnp.int32, s.shape, 1)
    s = jnp.where(c // CHUNK <= r // CHUNK, s, NEG_INF)
    _, l, acc = update(carry, s, v_ref[rows, :])
    o_ref[...] = _attn_finish(acc, l, tq, _diff_lambda(lam_ref, lam_init), g_ref[...], lam_init)


def _attn_prompt(lam_vecs, subln_g, qb, kb, vb, *, batch, seq, lam_init):
    tq = ATTN_Q_ROWS
    nq = seq // tq
    T = batch * seq
    qspec = pl.BlockSpec((tq, HEAD_W), lambda b, h, i: (b * nq + i, h))
    kvspec = pl.BlockSpec((seq, HEAD_W), lambda b, h, i: (b, h))
    return pl.pallas_call(
        functools.partial(_attn_prompt_kernel, tq=tq, lam_init=lam_init),
        grid=(batch, ATT_HEADS, nq),
        in_specs=[_const_spec((4, ATT_HD)), _const_spec((1, HEAD_W)), qspec, kvspec, kvspec],
        out_specs=qspec,
        out_shape=jax.ShapeDtypeStruct((T, ATT_WIDTH), BF16),
        compiler_params=_cparams(("parallel", "parallel", "arbitrary")),
        name="attn_prompt",
    )(lam_vecs, subln_g, qb, kb, vb)


def _attn_sample_kernel(lam_ref, g_ref, q_ref, kp_ref, vp_ref, kn_ref, vn_ref, o_ref, *, tq, lam_init):
    qq = _stack_maps(q_ref[...])
    s_past = _qk(qq, kp_ref[...].astype(BF16))
    s_new = _qk(qq, kn_ref[...])
    m = jnp.maximum(jnp.max(s_past, axis=1, keepdims=True), jnp.max(s_new, axis=1, keepdims=True))
    p_past = jnp.exp(s_past - m)
    p_new = jnp.exp(s_new - m)
    l = jnp.sum(p_past, axis=1, keepdims=True) + jnp.sum(p_new, axis=1, keepdims=True)
    acc = _dot(p_past.astype(BF16), vp_ref[...].astype(BF16)) + _dot(p_new.astype(BF16), vn_ref[...])
    o_ref[...] = _attn_finish(acc, l, tq, _diff_lambda(lam_ref, lam_init), g_ref[...], lam_init)


def _attn_sample(lam_vecs, subln_g, qb, k_past, v_past, kb, vb, *, batch, seq, past, lam_init):
    new = pl.BlockSpec((seq, HEAD_W), lambda b, h: (b, h))
    old = pl.BlockSpec((past, HEAD_W), lambda b, h: (b, h))
    return pl.pallas_call(
        functools.partial(_attn_sample_kernel, tq=seq, lam_init=lam_init),
        grid=(batch, ATT_HEADS),
        in_specs=[_const_spec((4, ATT_HD)), _const_spec((1, HEAD_W)), new, old, old, new, new],
        out_specs=new,
        out_shape=jax.ShapeDtypeStruct((batch * seq, ATT_WIDTH), BF16),
        compiler_params=_cparams(("parallel", "parallel")),
        name="attn_sample",
    )(lam_vecs, subln_g, qb, k_past, v_past, kb, vb)


def _ssm_kernel(u_ref, h0_ref, a_ref, b_ref, cr_ref, ci_ref,
                d_ref, wg_ref, bg_ref, s_ref, hout_ref, bu_ref, h_ref, tm_ref, *, steps, streams):
    n_state = a_ref.shape[1]
    n_blocks = b_ref.shape[0]
    in_w = b_ref.shape[1]
    st_w = n_state // n_blocks

    @pl.when(pl.program_id(0) == 0)
    def _():
        h_ref[...] = h0_ref[...]

    n_chunks = tm_ref.shape[0]
    chunk = lambda c: slice(c * LANES, (c + 1) * LANES)
    for b in range(streams):
        for c in range(n_chunks):
            tm_ref[c, pl.ds(b, steps, stride=streams), :] = u_ref[b, :, chunk(c)]
    u = jnp.concatenate([tm_ref[c] for c in range(n_chunks)], axis=1)
    u_b = u.astype(BF16)
    for gb in range(n_blocks):
        cols = slice(gb * in_w, (gb + 1) * in_w)
        bu = _dot(u_b[:, cols], b_ref[gb])
        bu_ref[:, gb * st_w:(gb + 1) * st_w] = bu[:, :st_w]
        bu_ref[:, n_state + gb * st_w:n_state + (gb + 1) * st_w] = bu[:, st_w:]

    for c in range(n_state // SCAN_COLS):
        re = slice(c * SCAN_COLS, (c + 1) * SCAN_COLS)
        im = slice(n_state + c * SCAN_COLS, n_state + (c + 1) * SCAN_COLS)
        ar = jnp.broadcast_to(a_ref[0:1, re], (streams, SCAN_COLS))
        ai = jnp.broadcast_to(a_ref[1:2, re], (streams, SCAN_COLS))

        def step(t, carry):
            hr, hi = carry
            rows = pl.ds(pl.multiple_of(t * streams, streams), streams)
            nr = ar * hr - ai * hi + bu_ref[rows, re]
            ni = ar * hi + ai * hr + bu_ref[rows, im]
            bu_ref[rows, re] = nr
            bu_ref[rows, im] = ni
            return nr, ni

        hr, hi = lax.fori_loop(0, steps, step, (h_ref[0, :, re], h_ref[1, :, re]), unroll=4)
        h_ref[0, :, re] = hr
        h_ref[1, :, re] = hi
    hout_ref[...] = h_ref[...]

    ys = []
    for gb in range(n_blocks):
        hr = bu_ref[:, gb * st_w:(gb + 1) * st_w].astype(BF16)
        hi = bu_ref[:, n_state + gb * st_w:n_state + (gb + 1) * st_w].astype(BF16)
        ys.append(_dot(hr, cr_ref[gb]) + _dot(hi, ci_ref[gb]))
    y = jax.nn.gelu(jnp.concatenate(ys, axis=1) + d_ref[...] * u)
    z = _dot(y.astype(BF16), wg_ref[...]) + bg_ref[...]
    s = y * jax.nn.sigmoid(z)
    for c in range(n_chunks):
        tm_ref[c] = s[:, chunk(c)]
    for b in range(streams):
        s_ref[b] = jnp.concatenate([tm_ref[c, pl.ds(b, steps, stride=streams), :] for c in range(n_chunks)],
                                   axis=1).astype(BF16)


def _ssm(u, h0, a, b, cr, ci, d, wg, bg, *, steps):
    streams, length, width = u.shape
    rows = steps * streams
    n_state = a.shape[1]
    blk = pl.BlockSpec((streams, steps, width), lambda i: (0, i, 0))
    consts = [h0, a, b, cr, ci, d, wg, bg]
    return pl.pallas_call(
        functools.partial(_ssm_kernel, steps=steps, streams=streams),
        grid=(length // steps,),
        in_specs=[blk] + [_const_spec(c.shape) for c in consts],
        out_specs=[blk, _const_spec(h0.shape)],
        out_shape=[jax.ShapeDtypeStruct(u.shape, BF16), jax.ShapeDtypeStruct(h0.shape, F32)],
        scratch_shapes=[pltpu.VMEM((rows, 2 * n_state), F32), pltpu.VMEM(h0.shape, F32),
                        pltpu.VMEM((width // LANES, rows, LANES), F32)],
        compiler_params=_cparams(("arbitrary",)),
        name="ssm",
    )(u, *consts)


def _ssm_params(a_re, a_im, log_dt, b_re, b_im, c_re, c_im):
    G, N, P = b_re.shape
    gb_n = SSM_GROUP_BLOCK
    nb = G // gb_n
    dt = jnp.exp(log_dt.astype(F32))[:, None]
    ar, ai = a_re.astype(F32), a_im.astype(F32)
    mag = jnp.exp(dt * ar)
    abar_r = mag * jnp.cos(dt * ai)
    abar_i = mag * jnp.sin(dt * ai)
    den = ar * ar + ai * ai
    nr, ni = abar_r - 1.0, abar_i
    coef_r = (nr * ar + ni * ai) / den
    coef_i = (ni * ar - nr * ai) / den
    br, bi = b_re.astype(F32), b_im.astype(F32)
    bbar_r = coef_r[..., None] * br - coef_i[..., None] * bi
    bbar_i = coef_r[..., None] * bi + coef_i[..., None] * br
    eye = jnp.eye(gb_n, dtype=F32)

    def in_block(bb):
        w = bb.reshape(nb, gb_n, N, P).transpose(0, 1, 3, 2)
        return jnp.einsum('bgpn,gh->bgphn', w, eye).reshape(nb, gb_n * P, gb_n * N)

    def out_block(cc):
        w = cc.reshape(nb, gb_n, P, N).transpose(0, 1, 3, 2)
        return jnp.einsum('bgnp,gh->bgnhp', w, eye).reshape(nb, gb_n * N, gb_n * P)

    b_bd = jnp.concatenate([in_block(bbar_r), in_block(bbar_i)], axis=2)
    a = jnp.stack([abar_r.reshape(-1), abar_i.reshape(-1)])
    return a, b_bd.astype(BF16), out_block(c_re.astype(F32)).astype(BF16), (-out_block(c_im.astype(F32))).astype(BF16)


def _outproj_kernel(x_ref, o_ref, s_ref, wo_ref, g2_ref, wq_ref, keys_ref, x1_ref, xn_ref, sc_ref):
    aw = o_ref.shape[1]
    x1 = x_ref[...] + _dot(o_ref[...], wo_ref[:aw, :]) + _dot(s_ref[...], wo_ref[aw:, :])
    x1_ref[...] = x1
    ms = jnp.mean(x1 * x1, axis=-1, keepdims=True)
    xn = x1 * lax.rsqrt(ms + EPS) * g2_ref[...]
    xn_ref[...] = xn
    q = _dot(xn.astype(BF16), wq_ref[...])
    for j in range(keys_ref.shape[0]):
        qj = q[:, j * PEER_HALF:(j + 1) * PEER_HALF].astype(BF16)
        sc_ref[j] = _qk(keys_ref[j], qj)


def _outproj(x2d, o_b, s_b, wo_bf, g2, wq_bf, keys_bf, *, tm):
    T, D = x2d.shape
    nset = keys_bf.shape[0]
    row = lambda w: pl.BlockSpec((tm, w), lambda i: (i, 0))
    return pl.pallas_call(
        _outproj_kernel,
        grid=(T // tm,),
        in_specs=[row(D), row(o_b.shape[1]), row(s_b.shape[1]), _const_spec(wo_bf.shape), _const_spec((1, D)),
                  _const_spec(wq_bf.shape), _const_spec(keys_bf.shape)],
        out_specs=[row(D), row(D), pl.BlockSpec((nset, PEER_KEYS, tm), lambda i: (0, 0, i))],
        out_shape=[jax.ShapeDtypeStruct((T, D), F32), jax.ShapeDtypeStruct((T, D), F32),
                   jax.ShapeDtypeStruct((nset, PEER_KEYS, T), F32)],
        compiler_params=_cparams(("parallel",)),
        name="outproj",
    )(x2d, o_b, s_b, wo_bf, g2, wq_bf, keys_bf)


def _extract_topk(vals, pos, payload, k):
    n, tt = vals.shape
    slot = lax.broadcasted_iota(jnp.int32, (k, tt), 0)
    out_v = jnp.zeros((k, tt), F32)
    out_p = jnp.zeros((k, tt), F32)
    for r in range(k):
        m = jnp.max(vals, axis=0, keepdims=True)
        first = jnp.min(jnp.where(vals == m, pos, jnp.inf), axis=0, keepdims=True)
        hit = pos == first
        picked = first if payload is None else jnp.max(jnp.where(hit, payload, -1.0), axis=0, keepdims=True)
        out_v = jnp.where(slot == r, m, out_v)
        out_p = jnp.where(slot == r, picked, out_p)
        vals = jnp.where(hit, -jnp.inf, vals)
    return out_v, out_p


def _staircase_candidates(a, ia, b, ib, k):
    rows = SUBLANES
    row = lax.broadcasted_iota(jnp.int32, (rows, a.shape[1]), 0)
    rowf = row.astype(F32)
    vals, pos, eid = [], [], []

    def add(v, p, e, valid=None):
        vals.append(v if valid is None else jnp.where(valid, v, -jnp.inf))
        pos.append(p)
        eid.append(e)

    i = 0
    while k // (i + 1) >= rows:
        for j0 in range(0, k // (i + 1), rows):
            add(a[i:i + 1] + b[j0:j0 + rows], float(i * k + j0) + rowf,
                ia[i:i + 1] * float(PEER_KEYS) + ib[j0:j0 + rows])
        i += 1
    while k // (i + 1) > 1:
        add(a[i:i + 1] + b[:rows], float(i * k) + rowf, ia[i:i + 1] * float(PEER_KEYS) + ib[:rows],
            valid=row < k // (i + 1))
        i += 1
    for i0 in range(i, k, rows):
        add(a[i0:i0 + rows] + b[0:1], (float(i0) + rowf) * float(k), ia[i0:i0 + rows] * float(PEER_KEYS) + ib[0:1])
    return tuple(jnp.concatenate(x, axis=0) for x in (vals, pos, eid))


def _topk_kernel(sc_ref, idx_ref, gate_ref):
    k = PEER_TOPK
    tt = sc_ref.shape[2]
    key_id = lax.broadcasted_iota(jnp.int32, (PEER_KEYS, tt), 0).astype(F32)

    def head(h, _):
        v0, i0 = _extract_topk(sc_ref[2 * h], key_id, None, k)
        v1, i1 = _extract_topk(sc_ref[2 * h + 1], key_id, None, k)
        fv, fe = _extract_topk(*_staircase_candidates(v0, i0, v1, i1, k), k)
        e = jnp.exp(fv - fv[0:1])
        rows = pl.ds(pl.multiple_of(h * k, k), k)
        gate_ref[rows, :] = e / jnp.sum(e, axis=0, keepdims=True)
        idx_ref[rows, :] = fe.astype(jnp.int32)
        return 0

    lax.fori_loop(0, PEER_HEADS, head, 0, unroll=2)


def _topk(scores):
    nset, nkeys, T = scores.shape
    tt = TOPK_TOKENS
    out = pl.BlockSpec((PEER_HEADS * PEER_TOPK, tt), lambda i: (0, i))
    return pl.pallas_call(
        _topk_kernel,
        grid=(T // tt,),
        in_specs=[pl.BlockSpec((nset, nkeys, tt), lambda i: (0, 0, i))],
        out_specs=[out, out],
        out_shape=[jax.ShapeDtypeStruct((PEER_HEADS * PEER_TOPK, T), jnp.int32),
                   jax.ShapeDtypeStruct((PEER_HEADS * PEER_TOPK, T), F32)],
        compiler_params=_cparams(("parallel",)),
        name="peer_topk",
    )(scores)


def _fold_pairs(vs, shift, keep):
    return [jnp.where(keep, a, b) + pltpu.roll(jnp.where(keep, b, a), shift, 0) for a, b in zip(vs[0::2], vs[1::2])]


def _expert_kernel(idx_ref, xn_ref, x1_ref, gate_ref, uv_hbm, y_ref, buf, sem, cb_ref, *, tokens, slots):
    n_sel = buf.shape[1]
    half = SUBLANES
    lane = lax.broadcasted_iota(jnp.int32, (n_sel, tokens), 1)
    sub = lax.broadcasted_iota(jnp.int32, (SUBLANES, LANES), 0)
    keep1, keep2, keep4 = sub % 2 == 0, sub % 4 < 2, sub < 4

    def issue(t, ks):
        slot = t % slots
        for k in ks:
            pltpu.make_async_copy(uv_hbm.at[idx_ref[t, k]], buf.at[slot, k], sem.at[slot]).start(priority=k % 2)

    def wait(t):
        slot = t % slots
        pltpu.make_async_copy(uv_hbm.at[pl.ds(0, n_sel)], buf.at[slot], sem.at[slot]).wait()

    n_groups = n_sel // SUBLANES
    per_chunk = n_sel // (2 * n_groups)

    def compute(t, t_next):
        slot = t % slots
        row = pl.ds(t, 1)
        chunk = lambda r: slice(r * LANES, (r + 1) * LANES)
        xrow = xn_ref[row, :]
        xa = jnp.concatenate([xrow[:, chunk(r)] for r in range(half)], axis=0)
        xb = jnp.concatenate([xrow[:, chunk(half + r)] for r in range(half)], axis=0)
        folded = []
        for g in range(n_groups):
            if t_next is not None:
                issue(t_next, range(g * per_chunk, (g + 1) * per_chunk))
            ps = []
            for k in range(g * SUBLANES, (g + 1) * SUBLANES):
                w = buf[slot, k, 0].astype(F32)
                ps.append(w[:half] * xa + w[half:] * xb)
            ps = _fold_pairs(_fold_pairs(_fold_pairs(ps, 1, keep1), 2, keep2), 4, keep4)
            folded.append(ps[0])
        act = jnp.sum(jnp.concatenate(folded, axis=0), axis=1, keepdims=True)
        gate = jnp.sum(jnp.where(lane == t, gate_ref[...], 0.0), axis=1, keepdims=True)
        cb_ref[...] = jnp.broadcast_to(gate * jax.nn.gelu(act), cb_ref.shape)
        n_acc = 4
        acc_a = [jnp.zeros((half, LANES), F32)] * n_acc
        acc_b = [jnp.zeros((half, LANES), F32)] * n_acc
        for g in range(n_groups):
            if t_next is not None:
                issue(t_next, range((n_groups + g) * per_chunk, (n_groups + g + 1) * per_chunk))
            for k in range(g * SUBLANES, (g + 1) * SUBLANES):
                c = jnp.broadcast_to(cb_ref[pl.ds(k, 1), :], (half, LANES))
                w = buf[slot, k, 1].astype(F32)
                acc_a[k % n_acc] = acc_a[k % n_acc] + c * w[:half]
                acc_b[k % n_acc] = acc_b[k % n_acc] + c * w[half:]
        out_a = (acc_a[0] + acc_a[1]) + (acc_a[2] + acc_a[3])
        out_b = (acc_b[0] + acc_b[1]) + (acc_b[2] + acc_b[3])
        out_row = jnp.concatenate([o[r:r + 1] for o in (out_a, out_b) for r in range(half)], axis=1)
        y_ref[row, :] = x1_ref[row, :] + out_row

    ahead = slots - 1

    def fill(t, c):
        issue(t, range(n_sel))
        return c

    def steady(t, c):
        wait(t)
        compute(t, t + ahead)
        return c

    def drain(t, c):
        wait(t)
        compute(t, None)
        return c

    lax.fori_loop(0, ahead, fill, 0)
    lax.fori_loop(0, tokens - ahead, steady, 0)
    lax.fori_loop(tokens - ahead, tokens, drain, 0)


def _experts(idx_tok, xn, x1, gates, uv):
    T, D = xn.shape
    n_sel = idx_tok.shape[1]
    tt = EXPERT_TOKENS
    row = pl.BlockSpec((tt, D), lambda i: (i, 0))
    return pl.pallas_call(
        functools.partial(_expert_kernel, tokens=tt, slots=EXPERT_SLOTS),
        grid=(T // tt,),
        in_specs=[pl.BlockSpec((tt, n_sel), lambda i: (i, 0), memory_space=pltpu.SMEM), row, row,
                  pl.BlockSpec((n_sel, tt), lambda i: (0, i)), pl.BlockSpec(memory_space=pl.ANY)],
        out_specs=row,
        out_shape=jax.ShapeDtypeStruct((T, D), F32),
        scratch_shapes=[pltpu.VMEM((EXPERT_SLOTS, n_sel) + uv.shape[1:], uv.dtype),
                        pltpu.SemaphoreType.DMA((EXPERT_SLOTS,)), pltpu.VMEM((n_sel, LANES), F32)],
        compiler_params=_cparams(("arbitrary",)),
        name="peer_experts",
    )(idx_tok, xn, x1, gates, uv)


def _rope_tables(pos):
    half = ROPE_DIM // 2
    inv_freq = ROPE_THETA ** (-jnp.arange(half, dtype=F32) * (2.0 / ROPE_DIM))
    ang = pos.astype(F32)[:, None] * inv_freq[None, :]
    cos, sin = jnp.cos(ang), jnp.sin(ang)
    n = pos.shape[0]
    pad = jnp.zeros((n, ATT_HD - ROPE_DIM), F32)
    zero = jnp.zeros((n, half), F32)
    c = jnp.concatenate([cos, cos, pad + 1.0], axis=1)
    sa = jnp.concatenate([-sin, zero, pad], axis=1)
    sb = jnp.concatenate([zero, sin, pad], axis=1)
    return tuple(jnp.tile(t, (1, LANES // ATT_HD)) for t in (c, sa, sb))


def _layer(x, pos, past_k, past_v, h0_re, h0_im, lam_init, p):
    B, L, D = x.shape
    T = B * L
    x2d = x.reshape(T, D)
    tm = min(INPROJ_ROWS, L)
    cos, sa, sb = _rope_tables(pos)
    qb, kf, kb, vf, vb, u = _inproj(x2d, p['g1'], p['w_in'], p['bd'], p['qg'], p['kg'], cos, sa, sb,
                                   tm=tm, pos_blocks=L // tm)
    if past_k is None:
        o = _attn_prompt(p['lam'], p['subln'], qb, kb, vb, batch=B, seq=L, lam_init=lam_init)
    else:
        past = past_k.shape[1]
        o = _attn_sample(p['lam'], p['subln'], qb, past_k.reshape(B * past, ATT_WIDTH),
                         past_v.reshape(B * past, ATT_WIDTH), kb, vb, batch=B, seq=L, past=past,
                         lam_init=lam_init)
    n_state = p['a'].shape[1]
    if h0_re is None:
        h0 = jnp.zeros((2, B, n_state), F32)
    else:
        h0 = jnp.stack([h0_re.reshape(B, n_state), h0_im.reshape(B, n_state)]).astype(F32)
    ssm_w = u.shape[1]
    s, h_fin = _ssm(u.reshape(B, L, ssm_w), h0, p['a'], p['b'], p['cr'], p['ci'], p['d'], p['wg'], p['bg'],
                    steps=min(SSM_STEPS, L))
    s = s.reshape(T, ssm_w)
    x1, xn, scores = _outproj(x2d, o, s, p['w_out'], p['g2'], p['wq'], p['keys'], tm=min(OUT_ROWS, T))
    idx_t, gates = _topk(scores)
    y = _experts(idx_t.T, xn, x1, gates, p['uv'])
    G = n_state // SSM_N
    return (y.reshape(B, L, D), kf.reshape(B, L, ATT_HEADS, 2, ATT_HD), vf.reshape(B, L, ATT_HEADS, HEAD_W),
            h_fin[0].reshape(B, G, SSM_N), h_fin[1].reshape(B, G, SSM_N))


def kernel(x_prompt, x_sample, cache_k, cache_v, state_ssm_re, state_ssm_im, norm1_g, w_in, q_norm_g, k_norm_g, lambda_q1, lambda_k1, lambda_q2, lambda_k2, subln_g, ssm_a_re, ssm_a_im, ssm_log_dt, ssm_b_re, ssm_b_im, ssm_c_re, ssm_c_im, ssm_d, w_glu, b_glu, w_out, norm2_g, w_peer_q, peer_keys, peer_u, peer_v):
    depth = w_in.shape[0]
    Lp, Ls, past = x_prompt.shape[1], x_sample.shape[1], cache_k.shape[2]
    pos_p = jnp.arange(Lp, dtype=jnp.int32)
    pos_s = past + jnp.arange(Ls, dtype=jnp.int32)
    lane = jnp.arange(LANES)
    bd = (lane[:, None] // ATT_HD == lane[None, :] // ATT_HD).astype(BF16)
    yp, ys = x_prompt, x_sample
    outs = [[] for _ in range(8)]
    for layer in range(depth):
        lam_init = 0.8 - 0.6 * math.exp(-0.3 * layer)
        a, b, cr, ci = _ssm_params(ssm_a_re[layer], ssm_a_im[layer], ssm_log_dt[layer], ssm_b_re[layer],
                                   ssm_b_im[layer], ssm_c_re[layer], ssm_c_im[layer])
        slab_rows = peer_u.shape[-1] // LANES
        p = dict(
            g1=norm1_g[layer][None].astype(F32), w_in=w_in[layer].astype(BF16), bd=bd,
            qg=jnp.tile(q_norm_g[layer].astype(F32), LANES // ATT_HD)[None],
            kg=jnp.tile(k_norm_g[layer].astype(F32), LANES // ATT_HD)[None],
            lam=jnp.stack([lambda_q1[layer], lambda_k1[layer], lambda_q2[layer], lambda_k2[layer]]).astype(F32),
            subln=subln_g[layer][None].astype(F32),
            a=a, b=b, cr=cr, ci=ci, d=ssm_d[layer][None].astype(F32), wg=w_glu[layer].astype(BF16),
            bg=b_glu[layer][None].astype(F32), w_out=w_out[layer].astype(BF16),
            g2=norm2_g[layer][None].astype(F32), wq=w_peer_q[layer].astype(BF16),
            keys=peer_keys[layer].reshape(PEER_HEADS * 2, PEER_KEYS, PEER_HALF).astype(BF16),
            uv=jnp.stack([peer_u[layer].astype(BF16).reshape(-1, slab_rows, LANES),
                          peer_v[layer].astype(BF16).reshape(-1, slab_rows, LANES)], axis=1))
        yp, kp, vp, hrp, hip = _layer(yp, pos_p, None, None, None, None, lam_init, p)
        ys, kk, vv, hrs, his = _layer(ys, pos_s, cache_k[layer], cache_v[layer], state_ssm_re[layer],
                                      state_ssm_im[layer], lam_init, p)
        for lst, val in zip(outs, (kp, vp, hrp, hip, kk, vv, hrs, his)):
            lst.append(val)
    return (yp, ys) + tuple(jnp.stack(l) for l in outs)
```

```python
import functools
import math

import jax
import jax.numpy as jnp
from jax import lax
from jax.experimental import pallas as pl
from jax.experimental.pallas import tpu as pltpu

F32 = jnp.float32
BF16 = jnp.bfloat16

CHUNK = 64
ATT_HEADS = 8
ATT_HD = 64
HEAD_W = 2 * ATT_HD
ATT_WIDTH = ATT_HEADS * HEAD_W
ROPE_DIM = ATT_HD // 4
ROPE_THETA = 500000.0
SSM_P = 16
SSM_N = 64
PEER_HEADS = 8
PEER_KEYS = 128
PEER_HALF = 128
PEER_TOPK = 16
EPS = 1e-6
NEG_INF = -1e30

LANES = 128
SUBLANES = 8
VMEM_LIMIT_BYTES = 56 * 1024 * 1024

INPROJ_ROWS = 256
ATTN_Q_ROWS = 512
ATTN_K_ROWS = 512
SSM_STEPS = 32
SSM_GROUP_BLOCK = 16
SCAN_COLS = 512
OUT_ROWS = 256
TOPK_TOKENS = 128
EXPERT_TOKENS = 128
EXPERT_SLOTS = 8


def _cparams(sem):
    return pltpu.CompilerParams(dimension_semantics=sem, vmem_limit_bytes=VMEM_LIMIT_BYTES)


def _const_spec(shape):
    nd = len(shape)
    return pl.BlockSpec(shape, lambda *_: (0,) * nd, pipeline_mode=pl.Buffered(1))


def _split_bf16(a):
    hi = a.astype(BF16)
    lo = (a - hi.astype(F32)).astype(BF16)
    return hi, lo


def _dot(a, b):
    return jnp.dot(a, b, preferred_element_type=F32)


def _inproj_kernel(x_ref, g1_ref, w_ref, bd_ref, qg_ref, kg_ref, cos_ref, sa_ref, sb_ref,
                   qb_ref, kf_ref, kb_ref, vf_ref, vb_ref, u_ref):
    x = x_ref[...]
    ms = jnp.mean(x * x, axis=-1, keepdims=True)
    xn = (x * lax.rsqrt(ms + EPS) * g1_ref[...]).astype(BF16)
    proj = _dot(xn, w_ref[...])
    bd = bd_ref[...]
    cos, sa, sb = cos_ref[...], sa_ref[...], sb_ref[...]

    def head_norm_rope(t, g):
        hi, lo = _split_bf16(t * t)
        ssq = _dot(hi, bd) + _dot(lo, bd)
        n = t * lax.rsqrt(ssq * (1.0 / ATT_HD) + EPS) * g
        return n * cos + pltpu.roll(n, LANES - ROPE_DIM // 2, 1) * sa + pltpu.roll(n, ROPE_DIM // 2, 1) * sb

    for h in range(ATT_HEADS):
        c0 = h * HEAD_W
        q = head_norm_rope(proj[:, c0:c0 + HEAD_W], qg_ref[...])
        qb_ref[:, c0:c0 + HEAD_W] = (q * (ATT_HD ** -0.5)).astype(BF16)
        k = head_norm_rope(proj[:, ATT_WIDTH + c0:ATT_WIDTH + c0 + HEAD_W], kg_ref[...])
        kf_ref[:, c0:c0 + HEAD_W] = k
        kb_ref[:, c0:c0 + HEAD_W] = k.astype(BF16)
    v = proj[:, 2 * ATT_WIDTH:3 * ATT_WIDTH]
    vf_ref[...] = v
    vb_ref[...] = v.astype(BF16)
    u_ref[...] = proj[:, 3 * ATT_WIDTH:]


def _inproj(x2d, g1, w_bf, bd, qg, kg, cos, sa, sb, *, tm, pos_blocks):
    T, D = x2d.shape
    n_in = w_bf.shape[1]
    ssm_w = n_in - 3 * ATT_WIDTH
    row = lambda w: pl.BlockSpec((tm, w), lambda i: (i, 0))
    pos = pl.BlockSpec((tm, LANES), lambda i: (i % pos_blocks, 0))
    outs = [jax.ShapeDtypeStruct((T, ATT_WIDTH), BF16), jax.ShapeDtypeStruct((T, ATT_WIDTH), F32),
            jax.ShapeDtypeStruct((T, ATT_WIDTH), BF16), jax.ShapeDtypeStruct((T, ATT_WIDTH), F32),
            jax.ShapeDtypeStruct((T, ATT_WIDTH), BF16), jax.ShapeDtypeStruct((T, ssm_w), F32)]
    return pl.pallas_call(
        _inproj_kernel,
        grid=(T // tm,),
        in_specs=[row(D), _const_spec((1, D)), _const_spec((D, n_in)), _const_spec((LANES, LANES)),
                  _const_spec((1, LANES)), _const_spec((1, LANES)), pos, pos, pos],
        out_specs=[row(ATT_WIDTH)] * 5 + [row(ssm_w)],
        out_shape=outs,
        compiler_params=_cparams(("parallel",)),
        name="inproj",
    )(x2d, g1, w_bf, bd, qg, kg, cos, sa, sb)


def _diff_lambda(lam_ref, lam_init):
    lv = lam_ref[...]
    a = jnp.sum(lv[0:1] * lv[1:2], axis=-1, keepdims=True)
    b = jnp.sum(lv[2:3] * lv[3:4], axis=-1, keepdims=True)
    return jnp.exp(a) - jnp.exp(b) + lam_init


def _stack_maps(q):
    lane = lax.broadcasted_iota(jnp.int32, q.shape, 1)
    zero = jnp.zeros_like(q)
    return jnp.concatenate([jnp.where(lane < ATT_HD, q, zero), jnp.where(lane >= ATT_HD, q, zero)], axis=0)


def _attn_finish(acc, l, tq, lam, g, lam_init):
    o = acc[:tq] / l[:tq] - lam * (acc[tq:] / l[tq:])
    ms = jnp.mean(o * o, axis=-1, keepdims=True)
    return (o * lax.rsqrt(ms + EPS) * g * (1.0 - lam_init)).astype(BF16)


def _qk(qq, kb):
    return lax.dot_general(qq, kb, (((1,), (1,)), ((), ())), preferred_element_type=F32)


def _attn_prompt_kernel(lam_ref, g_ref, q_ref, k_ref, v_ref, o_ref, *, tq, tk, lam_init):
    i = pl.program_id(2)
    qq = _stack_maps(q_ref[...])

    def update(carry, s, vb):
        m, l, acc = carry
        m_new = jnp.maximum(m, jnp.max(s, axis=1, keepdims=True))
        alpha = jnp.exp(m - m_new)
        p = jnp.exp(s - m_new)
        l = alpha * l + jnp.sum(p, axis=1, keepdims=True)
        acc = alpha * acc + _dot(p.astype(BF16), vb)
        return m_new, l, acc

    def full_block(j, carry):
        rows = pl.ds(pl.multiple_of(j * tk, tk), tk)
        return update(carry, _qk(qq, k_ref[rows, :]), v_ref[rows, :])

    carry = (jnp.full((2 * tq, 1), NEG_INF, F32), jnp.zeros((2 * tq, 1), F32),
             jnp.zeros((2 * tq, HEAD_W), F32))
    q0 = i * tq
    k0 = (q0 // tk) * tk
    carry = lax.fori_loop(0, q0 // tk, full_block, carry)
    r = q0 + lax.broadcasted_iota(jnp.int32, (2 * tq, tk), 0) % tq
    c = k0 + lax.broadcasted_iota(jnp.int32, (2 * tq, tk), 1)
    for d in range(max(tq // tk, 1)):
        rows = pl.ds(pl.multiple_of(k0 + d * tk, tk), tk)
        s = jnp.where((c + d * tk) // CHUNK <= r // CHUNK, _qk(qq, k_ref[rows, :]), NEG_INF)
        carry = update(carry, s, v_ref[rows, :])
    _, l, acc = carry
    o_ref[...] = _attn_finish(acc, l, tq, _diff_lambda(lam_ref, lam_init), g_ref[...], lam_init)


def _attn_prompt(lam_vecs, subln_g, qb, kb, vb, *, batch, seq, lam_init):
    tq = ATTN_Q_ROWS
    nq = seq // tq
    T = batch * seq
    qspec = pl.BlockSpec((tq, HEAD_W), lambda b, h, i: (b * nq + i, h))
    kvspec = pl.BlockSpec((seq, HEAD_W), lambda b, h, i: (b, h))
    return pl.pallas_call(
        functools.partial(_attn_prompt_kernel, tq=tq, tk=ATTN_K_ROWS, lam_init=lam_init),
        grid=(batch, ATT_HEADS, nq),
        in_specs=[_const_spec((4, ATT_HD)), _const_spec((1, HEAD_W)), qspec, kvspec, kvspec],
        out_specs=qspec,
        out_shape=jax.ShapeDtypeStruct((T, ATT_WIDTH), BF16),
        compiler_params=_cparams(("parallel", "parallel", "arbitrary")),
        name="attn_prompt",
    )(lam_vecs, subln_g, qb, kb, vb)


def _attn_sample_kernel(lam_ref, g_ref, q_ref, kp_ref, vp_ref, kn_ref, vn_ref, o_ref, *, tq, lam_init):
    qq = _stack_maps(q_ref[...])
    s_past = _qk(qq, kp_ref[...].astype(BF16))
    s_new = _qk(qq, kn_ref[...])
    m = jnp.maximum(jnp.max(s_past, axis=1, keepdims=True), jnp.max(s_new, axis=1, keepdims=True))
    p_past = jnp.exp(s_past - m)
    p_new = jnp.exp(s_new - m)
    l = jnp.sum(p_past, axis=1, keepdims=True) + jnp.sum(p_new, axis=1, keepdims=True)
    acc = _dot(p_past.astype(BF16), vp_ref[...].astype(BF16)) + _dot(p_new.astype(BF16), vn_ref[...])
    o_ref[...] = _attn_finish(acc, l, tq, _diff_lambda(lam_ref, lam_init), g_ref[...], lam_init)


def _attn_sample(lam_vecs, subln_g, qb, k_past, v_past, kb, vb, *, batch, seq, past, lam_init):
    new = pl.BlockSpec((seq, HEAD_W), lambda b, h: (b, h))
    old = pl.BlockSpec((past, HEAD_W), lambda b, h: (b, h))
    return pl.pallas_call(
        functools.partial(_attn_sample_kernel, tq=seq, lam_init=lam_init),
        grid=(batch, ATT_HEADS),
        in_specs=[_const_spec((4, ATT_HD)), _const_spec((1, HEAD_W)), new, old, old, new, new],
        out_specs=new,
        out_shape=jax.ShapeDtypeStruct((batch * seq, ATT_WIDTH), BF16),
        compiler_params=_cparams(("parallel", "parallel")),
        name="attn_sample",
    )(lam_vecs, subln_g, qb, k_past, v_past, kb, vb)


def _ssm_kernel(u_ref, h0_ref, a_ref, b_ref, cr_ref, ci_ref,
                d_ref, wg_ref, bg_ref, s_ref, hout_ref, bu_ref, h_ref, tm_ref, *, steps, streams):
    n_state = a_ref.shape[1]
    n_blocks = b_ref.shape[0]
    in_w = b_ref.shape[1]
    st_w = n_state // n_blocks

    @pl.when(pl.program_id(0) == 0)
    def _():
        h_ref[...] = h0_ref[...]

    n_chunks = tm_ref.shape[0]
    chunk = lambda c: slice(c * LANES, (c + 1) * LANES)
    for b in range(streams):
        for c in range(n_chunks):
            tm_ref[c, pl.ds(b, steps, stride=streams), :] = u_ref[b, :, chunk(c)]
    u = jnp.concatenate([tm_ref[c] for c in range(n_chunks)], axis=1)
    u_b = u.astype(BF16)
    for gb in range(n_blocks):
        cols = slice(gb * in_w, (gb + 1) * in_w)
        bu = _dot(u_b[:, cols], b_ref[gb])
        bu_ref[:, gb * st_w:(gb + 1) * st_w] = bu[:, :st_w]
        bu_ref[:, n_state + gb * st_w:n_state + (gb + 1) * st_w] = bu[:, st_w:]

    for c in range(n_state // SCAN_COLS):
        re = slice(c * SCAN_COLS, (c + 1) * SCAN_COLS)
        im = slice(n_state + c * SCAN_COLS, n_state + (c + 1) * SCAN_COLS)
        ar = jnp.broadcast_to(a_ref[0:1, re], (streams, SCAN_COLS))
        ai = jnp.broadcast_to(a_ref[1:2, re], (streams, SCAN_COLS))

        def step(t, carry):
            hr, hi = carry
            rows = pl.ds(pl.multiple_of(t * streams, streams), streams)
            nr = ar * hr - ai * hi + bu_ref[rows, re]
            ni = ar * hi + ai * hr + bu_ref[rows, im]
            bu_ref[rows, re] = nr
            bu_ref[rows, im] = ni
            return nr, ni

        hr, hi = lax.fori_loop(0, steps, step, (h_ref[0, :, re], h_ref[1, :, re]), unroll=4)
        h_ref[0, :, re] = hr
        h_ref[1, :, re] = hi
    hout_ref[...] = h_ref[...]

    ys = []
    for gb in range(n_blocks):
        hr = bu_ref[:, gb * st_w:(gb + 1) * st_w].astype(BF16)
        hi = bu_ref[:, n_state + gb * st_w:n_state + (gb + 1) * st_w].astype(BF16)
        ys.append(_dot(hr, cr_ref[gb]) + _dot(hi, ci_ref[gb]))
    y = jax.nn.gelu(jnp.concatenate(ys, axis=1) + d_ref[...] * u)
    z = _dot(y.astype(BF16), wg_ref[...]) + bg_ref[...]
    s = y * jax.nn.sigmoid(z)
    for c in range(n_chunks):
        tm_ref[c] = s[:, chunk(c)]
    for b in range(streams):
        s_ref[b] = jnp.concatenate([tm_ref[c, pl.ds(b, steps, stride=streams), :] for c in range(n_chunks)],
                                   axis=1).astype(BF16)


def _ssm(u, h0, a, b, cr, ci, d, wg, bg, *, steps):
    streams, length, width = u.shape
    rows = steps * streams
    n_state = a.shape[1]
    blk = pl.BlockSpec((streams, steps, width), lambda i: (0, i, 0))
    consts = [h0, a, b, cr, ci, d, wg, bg]
    return pl.pallas_call(
        functools.partial(_ssm_kernel, steps=steps, streams=streams),
        grid=(length // steps,),
        in_specs=[blk] + [_const_spec(c.shape) for c in consts],
        out_specs=[blk, _const_spec(h0.shape)],
        out_shape=[jax.ShapeDtypeStruct(u.shape, BF16), jax.ShapeDtypeStruct(h0.shape, F32)],
        scratch_shapes=[pltpu.VMEM((rows, 2 * n_state), F32), pltpu.VMEM(h0.shape, F32),
                        pltpu.VMEM((width // LANES, rows, LANES), F32)],
        compiler_params=_cparams(("arbitrary",)),
        name="ssm",
    )(u, *consts)


def _ssm_params(a_re, a_im, log_dt, b_re, b_im, c_re, c_im):
    G, N, P = b_re.shape
    gb_n = SSM_GROUP_BLOCK
    nb = G // gb_n
    dt = jnp.exp(log_dt.astype(F32))[:, None]
    ar, ai = a_re.astype(F32), a_im.astype(F32)
    mag = jnp.exp(dt * ar)
    abar_r = mag * jnp.cos(dt * ai)
    abar_i = mag * jnp.sin(dt * ai)
    den = ar * ar + ai * ai
    nr, ni = abar_r - 1.0, abar_i
    coef_r = (nr * ar + ni * ai) / den
    coef_i = (ni * ar - nr * ai) / den
    br, bi = b_re.astype(F32), b_im.astype(F32)
    bbar_r = coef_r[..., None] * br - coef_i[..., None] * bi
    bbar_i = coef_r[..., None] * bi + coef_i[..., None] * br
    eye = jnp.eye(gb_n, dtype=F32)

    def in_block(bb):
        w = bb.reshape(nb, gb_n, N, P).transpose(0, 1, 3, 2)
        return jnp.einsum('bgpn,gh->bgphn', w, eye).reshape(nb, gb_n * P, gb_n * N)

    def out_block(cc):
        w = cc.reshape(nb, gb_n, P, N).transpose(0, 1, 3, 2)
        return jnp.einsum('bgnp,gh->bgnhp', w, eye).reshape(nb, gb_n * N, gb_n * P)

    b_bd = jnp.concatenate([in_block(bbar_r), in_block(bbar_i)], axis=2)
    a = jnp.stack([abar_r.reshape(-1), abar_i.reshape(-1)])
    return a, b_bd.astype(BF16), out_block(c_re.astype(F32)).astype(BF16), (-out_block(c_im.astype(F32))).astype(BF16)


def _outproj_kernel(x_ref, o_ref, s_ref, wo_ref, g2_ref, wq_ref, keys_ref, x1_ref, xn_ref, sc_ref):
    aw = o_ref.shape[1]
    x1 = x_ref[...] + _dot(o_ref[...], wo_ref[:aw, :]) + _dot(s_ref[...], wo_ref[aw:, :])
    x1_ref[...] = x1
    ms = jnp.mean(x1 * x1, axis=-1, keepdims=True)
    xn = x1 * lax.rsqrt(ms + EPS) * g2_ref[...]
    xn_ref[...] = xn
    q = _dot(xn.astype(BF16), wq_ref[...])
    for j in range(keys_ref.shape[0]):
        qj = q[:, j * PEER_HALF:(j + 1) * PEER_HALF].astype(BF16)
        sc_ref[j] = _qk(keys_ref[j], qj)


def _outproj(x2d, o_b, s_b, wo_bf, g2, wq_bf, keys_bf, *, tm):
    T, D = x2d.shape
    nset = keys_bf.shape[0]
    row = lambda w: pl.BlockSpec((tm, w), lambda i: (i, 0))
    return pl.pallas_call(
        _outproj_kernel,
        grid=(T // tm,),
        in_specs=[row(D), row(o_b.shape[1]), row(s_b.shape[1]), _const_spec(wo_bf.shape), _const_spec((1, D)),
                  _const_spec(wq_bf.shape), _const_spec(keys_bf.shape)],
        out_specs=[row(D), row(D), pl.BlockSpec((nset, PEER_KEYS, tm), lambda i: (0, 0, i))],
        out_shape=[jax.ShapeDtypeStruct((T, D), F32), jax.ShapeDtypeStruct((T, D), F32),
                   jax.ShapeDtypeStruct((nset, PEER_KEYS, T), F32)],
        compiler_params=_cparams(("parallel",)),
        name="outproj",
    )(x2d, o_b, s_b, wo_bf, g2, wq_bf, keys_bf)


def _extract_topk(vals, pos, payload, k):
    n, tt = vals.shape
    slot = lax.broadcasted_iota(jnp.int32, (k, tt), 0)
    out_v = jnp.zeros((k, tt), F32)
    out_p = jnp.zeros((k, tt), F32)
    for r in range(k):
        m = jnp.max(vals, axis=0, keepdims=True)
        first = jnp.min(jnp.where(vals == m, pos, jnp.inf), axis=0, keepdims=True)
        hit = pos == first
        picked = first if payload is None else jnp.max(jnp.where(hit, payload, -1.0), axis=0, keepdims=True)
        out_v = jnp.where(slot == r, m, out_v)
        out_p = jnp.where(slot == r, picked, out_p)
        vals = jnp.where(hit, -jnp.inf, vals)
    return out_v, out_p


def _staircase_candidates(a, ia, b, ib, k):
    rows = SUBLANES
    row = lax.broadcasted_iota(jnp.int32, (rows, a.shape[1]), 0)
    rowf = row.astype(F32)
    vals, pos, eid = [], [], []

    def add(v, p, e, valid=None):
        vals.append(v if valid is None else jnp.where(valid, v, -jnp.inf))
        pos.append(p)
        eid.append(e)

    i = 0
    while k // (i + 1) >= rows:
        for j0 in range(0, k // (i + 1), rows):
            add(a[i:i + 1] + b[j0:j0 + rows], float(i * k + j0) + rowf,
                ia[i:i + 1] * float(PEER_KEYS) + ib[j0:j0 + rows])
        i += 1
    while k // (i + 1) > 1:
        add(a[i:i + 1] + b[:rows], float(i * k) + rowf, ia[i:i + 1] * float(PEER_KEYS) + ib[:rows],
            valid=row < k // (i + 1))
        i += 1
    for i0 in range(i, k, rows):
        add(a[i0:i0 + rows] + b[0:1], (float(i0) + rowf) * float(k), ia[i0:i0 + rows] * float(PEER_KEYS) + ib[0:1])
    return tuple(jnp.concatenate(x, axis=0) for x in (vals, pos, eid))


def _topk_kernel(sc_ref, idx_ref, gate_ref):
    k = PEER_TOPK
    tt = sc_ref.shape[2]
    key_id = lax.broadcasted_iota(jnp.int32, (PEER_KEYS, tt), 0).astype(F32)

    def head(h, _):
        v0, i0 = _extract_topk(sc_ref[2 * h], key_id, None, k)
        v1, i1 = _extract_topk(sc_ref[2 * h + 1], key_id, None, k)
        fv, fe = _extract_topk(*_staircase_candidates(v0, i0, v1, i1, k), k)
        e = jnp.exp(fv - fv[0:1])
        rows = pl.ds(pl.multiple_of(h * k, k), k)
        gate_ref[rows, :] = e / jnp.sum(e, axis=0, keepdims=True)
        idx_ref[rows, :] = fe.astype(jnp.int32)
        return 0

    lax.fori_loop(0, PEER_HEADS, head, 0, unroll=2)


def _topk(scores):
    nset, nkeys, T = scores.shape
    tt = TOPK_TOKENS
    out = pl.BlockSpec((PEER_HEADS * PEER_TOPK, tt), lambda i: (0, i))
    return pl.pallas_call(
        _topk_kernel,
        grid=(T // tt,),
        in_specs=[pl.BlockSpec((nset, nkeys, tt), lambda i: (0, 0, i))],
        out_specs=[out, out],
        out_shape=[jax.ShapeDtypeStruct((PEER_HEADS * PEER_TOPK, T), jnp.int32),
                   jax.ShapeDtypeStruct((PEER_HEADS * PEER_TOPK, T), F32)],
        compiler_params=_cparams(("parallel",)),
        name="peer_topk",
    )(scores)


def _fold_pairs(vs, shift, keep):
    return [jnp.where(keep, a, b) + pltpu.roll(jnp.where(keep, b, a), shift, 0) for a, b in zip(vs[0::2], vs[1::2])]


def _expert_kernel(idx_ref, xn_ref, x1_ref, gate_ref, uv_hbm, y_ref, buf, sem, cb_ref, *, tokens, slots):
    n_sel = buf.shape[1]
    half = SUBLANES
    lane = lax.broadcasted_iota(jnp.int32, (n_sel, tokens), 1)
    sub = lax.broadcasted_iota(jnp.int32, (SUBLANES, LANES), 0)
    keep1, keep2, keep4 = sub % 2 == 0, sub % 4 < 2, sub < 4

    def issue(t, ks):
        slot = t % slots
        for k in ks:
            pltpu.make_async_copy(uv_hbm.at[idx_ref[t, k]], buf.at[slot, k], sem.at[slot]).start(priority=k % 2)

    def wait(t):
        slot = t % slots
        pltpu.make_async_copy(uv_hbm.at[pl.ds(0, n_sel)], buf.at[slot], sem.at[slot]).wait()

    n_groups = n_sel // SUBLANES
    per_chunk = n_sel // (2 * n_groups)

    def compute(t, t_next):
        slot = t % slots
        row = pl.ds(t, 1)
        chunk = lambda r: slice(r * LANES, (r + 1) * LANES)
        xrow = xn_ref[row, :]
        xa = jnp.concatenate([xrow[:, chunk(r)] for r in range(half)], axis=0)
        xb = jnp.concatenate([xrow[:, chunk(half + r)] for r in range(half)], axis=0)
        folded = []
        for g in range(n_groups):
            ps = []
            for k in range(g * SUBLANES, (g + 1) * SUBLANES):
                if t_next is not None and k % 16 == 0:
                    issue(t_next, range(k // 2, k // 2 + 8))
                w = buf[slot, k, 0].astype(F32)
                ps.append(w[:half] * xa + w[half:] * xb)
            ps = _fold_pairs(_fold_pairs(_fold_pairs(ps, 1, keep1), 2, keep2), 4, keep4)
            folded.append(ps[0])
        act = jnp.sum(jnp.concatenate(folded, axis=0), axis=1, keepdims=True)
        gate = jnp.sum(jnp.where(lane == t, gate_ref[...], 0.0), axis=1, keepdims=True)
        cb_ref[...] = jnp.broadcast_to(gate * jax.nn.gelu(act), cb_ref.shape)
        n_acc = 4
        acc_a = [jnp.zeros((half, LANES), F32)] * n_acc
        acc_b = [jnp.zeros((half, LANES), F32)] * n_acc
        for g in range(n_groups):
            for k in range(g * SUBLANES, (g + 1) * SUBLANES):
                if t_next is not None and k % 16 == 0:
                    issue(t_next, range((n_sel + k) // 2, (n_sel + k) // 2 + 8))
                c = jnp.broadcast_to(cb_ref[pl.ds(k, 1), :], (half, LANES))
                w = buf[slot, k, 1].astype(F32)
                acc_a[k % n_acc] = acc_a[k % n_acc] + c * w[:half]
                acc_b[k % n_acc] = acc_b[k % n_acc] + c * w[half:]
        out_a = (acc_a[0] + acc_a[1]) + (acc_a[2] + acc_a[3])
        out_b = (acc_b[0] + acc_b[1]) + (acc_b[2] + acc_b[3])
        out_row = jnp.concatenate([o[r:r + 1] for o in (out_a, out_b) for r in range(half)], axis=1)
        y_ref[row, :] = x1_ref[row, :] + out_row

    ahead = slots - 1

    def fill(t, c):
        issue(t, range(n_sel))
        return c

    def steady(t, c):
        wait(t)
        compute(t, t + ahead)
        return c

    def drain(t, c):
        wait(t)
        compute(t, None)
        return c

    lax.fori_loop(0, ahead, fill, 0)
    lax.fori_loop(0, tokens - ahead, steady, 0)
    lax.fori_loop(tokens - ahead, tokens, drain, 0)


def _experts(idx_tok, xn, x1, gates, uv):
    T, D = xn.shape
    n_sel = idx_tok.shape[1]
    tt = EXPERT_TOKENS
    row = pl.BlockSpec((tt, D), lambda i: (i, 0))
    return pl.pallas_call(
        functools.partial(_expert_kernel, tokens=tt, slots=EXPERT_SLOTS),
        grid=(T // tt,),
        in_specs=[pl.BlockSpec((tt, n_sel), lambda i: (i, 0), memory_space=pltpu.SMEM), row, row,
                  pl.BlockSpec((n_sel, tt), lambda i: (0, i)), pl.BlockSpec(memory_space=pl.ANY)],
        out_specs=row,
        out_shape=jax.ShapeDtypeStruct((T, D), F32),
        scratch_shapes=[pltpu.VMEM((EXPERT_SLOTS, n_sel) + uv.shape[1:], uv.dtype),
                        pltpu.SemaphoreType.DMA((EXPERT_SLOTS,)), pltpu.VMEM((n_sel, LANES), F32)],
        compiler_params=_cparams(("arbitrary",)),
        name="peer_experts",
    )(idx_tok, xn, x1, gates, uv)


def _rope_tables(pos):
    half = ROPE_DIM // 2
    inv_freq = ROPE_THETA ** (-jnp.arange(half, dtype=F32) * (2.0 / ROPE_DIM))
    ang = pos.astype(F32)[:, None] * inv_freq[None, :]
    cos, sin = jnp.cos(ang), jnp.sin(ang)
    n = pos.shape[0]
    pad = jnp.zeros((n, ATT_HD - ROPE_DIM), F32)
    zero = jnp.zeros((n, half), F32)
    c = jnp.concatenate([cos, cos, pad + 1.0], axis=1)
    sa = jnp.concatenate([-sin, zero, pad], axis=1)
    sb = jnp.concatenate([zero, sin, pad], axis=1)
    return tuple(jnp.tile(t, (1, LANES // ATT_HD)) for t in (c, sa, sb))


def _layer(x, pos, past_k, past_v, h0_re, h0_im, lam_init, p):
    B, L, D = x.shape
    T = B * L
    x2d = x.reshape(T, D)
    tm = min(INPROJ_ROWS, L)
    cos, sa, sb = _rope_tables(pos)
    qb, kf, kb, vf, vb, u = _inproj(x2d, p['g1'], p['w_in'], p['bd'], p['qg'], p['kg'], cos, sa, sb,
                                   tm=tm, pos_blocks=L // tm)
    if past_k is None:
        o = _attn_prompt(p['lam'], p['subln'], qb, kb, vb, batch=B, seq=L, lam_init=lam_init)
    else:
        past = past_k.shape[1]
        o = _attn_sample(p['lam'], p['subln'], qb, past_k.reshape(B * past, ATT_WIDTH),
                         past_v.reshape(B * past, ATT_WIDTH), kb, vb, batch=B, seq=L, past=past,
                         lam_init=lam_init)
    n_state = p['a'].shape[1]
    if h0_re is None:
        h0 = jnp.zeros((2, B, n_state), F32)
    else:
        h0 = jnp.stack([h0_re.reshape(B, n_state), h0_im.reshape(B, n_state)]).astype(F32)
    ssm_w = u.shape[1]
    s, h_fin = _ssm(u.reshape(B, L, ssm_w), h0, p['a'], p['b'], p['cr'], p['ci'], p['d'], p['wg'], p['bg'],
                    steps=min(SSM_STEPS, L))
    s = s.reshape(T, ssm_w)
    x1, xn, scores = _outproj(x2d, o, s, p['w_out'], p['g2'], p['wq'], p['keys'], tm=min(OUT_ROWS, T))
    idx_t, gates = _topk(scores)
    y = _experts(idx_t.T, xn, x1, gates, p['uv'])
    G = n_state // SSM_N
    return (y.reshape(B, L, D), kf.reshape(B, L, ATT_HEADS, 2, ATT_HD), vf.reshape(B, L, ATT_HEADS, HEAD_W),
            h_fin[0].reshape(B, G, SSM_N), h_fin[1].reshape(B, G, SSM_N))


def kernel(x_prompt, x_sample, cache_k, cache_v, state_ssm_re, state_ssm_im, norm1_g, w_in, q_norm_g, k_norm_g, lambda_q1, lambda_k1, lambda_q2, lambda_k2, subln_g, ssm_a_re, ssm_a_im, ssm_log_dt, ssm_b_re, ssm_b_im, ssm_c_re, ssm_c_im, ssm_d, w_glu, b_glu, w_out, norm2_g, w_peer_q, peer_keys, peer_u, peer_v):
    depth = w_in.shape[0]
    Lp, Ls, past = x_prompt.shape[1], x_sample.shape[1], cache_k.shape[2]
    pos_p = jnp.arange(Lp, dtype=jnp.int32)
    pos_s = past + jnp.arange(Ls, dtype=jnp.int32)
    lane = jnp.arange(LANES)
    bd = (lane[:, None] // ATT_HD == lane[None, :] // ATT_HD).astype(BF16)
    yp, ys = x_prompt, x_sample
    outs = [[] for _ in range(8)]
    for layer in range(depth):
        lam_init = 0.8 - 0.6 * math.exp(-0.3 * layer)
        a, b, cr, ci = _ssm_params(ssm_a_re[layer], ssm_a_im[layer], ssm_log_dt[layer], ssm_b_re[layer],
                                   ssm_b_im[layer], ssm_c_re[layer], ssm_c_im[layer])
        slab_rows = peer_u.shape[-1] // LANES
        p = dict(
            g1=norm1_g[layer][None].astype(F32), w_in=w_in[layer].astype(BF16), bd=bd,
            qg=jnp.tile(q_norm_g[layer].astype(F32), LANES // ATT_HD)[None],
            kg=jnp.tile(k_norm_g[layer].astype(F32), LANES // ATT_HD)[None],
            lam=jnp.stack([lambda_q1[layer], lambda_k1[layer], lambda_q2[layer], lambda_k2[layer]]).astype(F32),
            subln=subln_g[layer][None].astype(F32),
            a=a, b=b, cr=cr, ci=ci, d=ssm_d[layer][None].astype(F32), wg=w_glu[layer].astype(BF16),
            bg=b_glu[layer][None].astype(F32), w_out=w_out[layer].astype(BF16),
            g2=norm2_g[layer][None].astype(F32), wq=w_peer_q[layer].astype(BF16),
            keys=peer_keys[layer].reshape(PEER_HEADS * 2, PEER_KEYS, PEER_HALF).astype(BF16),
            uv=jnp.stack([peer_u[layer].astype(BF16).reshape(-1, slab_rows, LANES),
                          peer_v[layer].astype(BF16).reshape(-1, slab_rows, LANES)], axis=1))
        yp, kp, vp, hrp, hip = _layer(yp, pos_p, None, None, None, None, lam_init, p)
        ys, kk, vv, hrs, his = _layer(ys, pos_s, cache_k[layer], cache_v[layer], state_ssm_re[layer],
                                      state_ssm_im[layer], lam_init, p)
        for lst, val in zip(outs, (kp, vp, hrp, hip, kk, vv, hrs, his)):
            lst.append(val)
    return (yp, ys) + tuple(jnp.stack(l) for l in outs)
```

```python
import functools
import math

import jax
import jax.numpy as jnp
from jax import lax
from jax.experimental import pallas as pl
from jax.experimental.pallas import tpu as pltpu

F32 = jnp.float32
BF16 = jnp.bfloat16

CHUNK = 64
ATT_HEADS = 8
ATT_HD = 64
HEAD_W = 2 * ATT_HD
ATT_WIDTH = ATT_HEADS * HEAD_W
ROPE_DIM = ATT_HD // 4
ROPE_THETA = 500000.0
SSM_P = 16
SSM_N = 64
PEER_HEADS = 8
PEER_KEYS = 128
PEER_HALF = 128
PEER_TOPK = 16
EPS = 1e-6
NEG_INF = -1e30

LANES = 128
SUBLANES = 8
VMEM_LIMIT_BYTES = 56 * 1024 * 1024

INPROJ_ROWS = 256
ATTN_Q_ROWS = 512
ATTN_K_ROWS = 512
SSM_STEPS = 32
SSM_GROUP_BLOCK = 16
SCAN_COLS = 512
OUT_ROWS = 256
TOPK_TOKENS = 128
EXPERT_TOKENS = 512
EXPERT_SLOTS = 8
PACK_ROWS = 256


def _cparams(sem):
    return pltpu.CompilerParams(dimension_semantics=sem, vmem_limit_bytes=VMEM_LIMIT_BYTES)


def _const_spec(shape):
    nd = len(shape)
    return pl.BlockSpec(shape, lambda *_: (0,) * nd, pipeline_mode=pl.Buffered(1))


def _split_bf16(a):
    hi = a.astype(BF16)
    lo = (a - hi.astype(F32)).astype(BF16)
    return hi, lo


def _dot(a, b):
    return jnp.dot(a, b, preferred_element_type=F32)


def _inproj_kernel(x_ref, g1_ref, w_ref, bd_ref, qg_ref, kg_ref, cos_ref, sa_ref, sb_ref,
                   qb_ref, kf_ref, kb_ref, vf_ref, vb_ref, u_ref):
    x = x_ref[...]
    ms = jnp.mean(x * x, axis=-1, keepdims=True)
    xn = (x * lax.rsqrt(ms + EPS) * g1_ref[...]).astype(BF16)
    proj = _dot(xn, w_ref[...])
    bd = bd_ref[...]
    cos, sa, sb = cos_ref[...], sa_ref[...], sb_ref[...]

    def head_norm_rope(t, g):
        hi, lo = _split_bf16(t * t)
        ssq = _dot(hi, bd) + _dot(lo, bd)
        n = t * lax.rsqrt(ssq * (1.0 / ATT_HD) + EPS) * g
        return n * cos + pltpu.roll(n, LANES - ROPE_DIM // 2, 1) * sa + pltpu.roll(n, ROPE_DIM // 2, 1) * sb

    for h in range(ATT_HEADS):
        c0 = h * HEAD_W
        q = head_norm_rope(proj[:, c0:c0 + HEAD_W], qg_ref[...])
        qb_ref[:, c0:c0 + HEAD_W] = (q * (ATT_HD ** -0.5)).astype(BF16)
        k = head_norm_rope(proj[:, ATT_WIDTH + c0:ATT_WIDTH + c0 + HEAD_W], kg_ref[...])
        kf_ref[:, c0:c0 + HEAD_W] = k
        kb_ref[:, c0:c0 + HEAD_W] = k.astype(BF16)
    v = proj[:, 2 * ATT_WIDTH:3 * ATT_WIDTH]
    vf_ref[...] = v
    vb_ref[...] = v.astype(BF16)
    u_ref[...] = proj[:, 3 * ATT_WIDTH:]


def _inproj(x2d, g1, w_bf, bd, qg, kg, cos, sa, sb, *, tm, pos_blocks):
    T, D = x2d.shape
    n_in = w_bf.shape[1]
    ssm_w = n_in - 3 * ATT_WIDTH
    row = lambda w: pl.BlockSpec((tm, w), lambda i: (i, 0))
    pos = pl.BlockSpec((tm, LANES), lambda i: (i % pos_blocks, 0))
    outs = [jax.ShapeDtypeStruct((T, ATT_WIDTH), BF16), jax.ShapeDtypeStruct((T, ATT_WIDTH), F32),
            jax.ShapeDtypeStruct((T, ATT_WIDTH), BF16), jax.ShapeDtypeStruct((T, ATT_WIDTH), F32),
            jax.ShapeDtypeStruct((T, ATT_WIDTH), BF16), jax.ShapeDtypeStruct((T, ssm_w), F32)]
    return pl.pallas_call(
        _inproj_kernel,
        grid=(T // tm,),
        in_specs=[row(D), _const_spec((1, D)), _const_spec((D, n_in)), _const_spec((LANES, LANES)),
                  _const_spec((1, LANES)), _const_spec((1, LANES)), pos, pos, pos],
        out_specs=[row(ATT_WIDTH)] * 5 + [row(ssm_w)],
        out_shape=outs,
        compiler_params=_cparams(("parallel",)),
        name="inproj",
    )(x2d, g1, w_bf, bd, qg, kg, cos, sa, sb)


def _diff_lambda(lam_ref, lam_init):
    lv = lam_ref[...]
    a = jnp.sum(lv[0:1] * lv[1:2], axis=-1, keepdims=True)
    b = jnp.sum(lv[2:3] * lv[3:4], axis=-1, keepdims=True)
    return jnp.exp(a) - jnp.exp(b) + lam_init


def _stack_maps(q):
    lane = lax.broadcasted_iota(jnp.int32, q.shape, 1)
    zero = jnp.zeros_like(q)
    return jnp.concatenate([jnp.where(lane < ATT_HD, q, zero), jnp.where(lane >= ATT_HD, q, zero)], axis=0)


def _attn_finish(acc, l, tq, lam, g, lam_init):
    o = acc[:tq] / l[:tq] - lam * (acc[tq:] / l[tq:])
    ms = jnp.mean(o * o, axis=-1, keepdims=True)
    return (o * lax.rsqrt(ms + EPS) * g * (1.0 - lam_init)).astype(BF16)


def _qk(qq, kb):
    return lax.dot_general(qq, kb, (((1,), (1,)), ((), ())), preferred_element_type=F32)


def _attn_prompt_kernel(lam_ref, g_ref, q_ref, k_ref, v_ref, o_ref, *, tq, tk, lam_init):
    i = pl.program_id(2)
    qq = _stack_maps(q_ref[...])

    def update(carry, s, vb):
        m, l, acc = carry
        m_new = jnp.maximum(m, jnp.max(s, axis=1, keepdims=True))
        alpha = jnp.exp(m - m_new)
        p = jnp.exp(s - m_new)
        l = alpha * l + jnp.sum(p, axis=1, keepdims=True)
        acc = alpha * acc + _dot(p.astype(BF16), vb)
        return m_new, l, acc

    def full_block(j, carry):
        rows = pl.ds(pl.multiple_of(j * tk, tk), tk)
        return update(carry, _qk(qq, k_ref[rows, :]), v_ref[rows, :])

    carry = (jnp.full((2 * tq, 1), NEG_INF, F32), jnp.zeros((2 * tq, 1), F32),
             jnp.zeros((2 * tq, HEAD_W), F32))
    q0 = i * tq
    k0 = (q0 // tk) * tk
    carry = lax.fori_loop(0, q0 // tk, full_block, carry)
    r = q0 + lax.broadcasted_iota(jnp.int32, (2 * tq, tk), 0) % tq
    c = k0 + lax.broadcasted_iota(jnp.int32, (2 * tq, tk), 1)
    for d in range(max(tq // tk, 1)):
        rows = pl.ds(pl.multiple_of(k0 + d * tk, tk), tk)
        s = jnp.where((c + d * tk) // CHUNK <= r // CHUNK, _qk(qq, k_ref[rows, :]), NEG_INF)
        carry = update(carry, s, v_ref[rows, :])
    _, l, acc = carry
    o_ref[...] = _attn_finish(acc, l, tq, _diff_lambda(lam_ref, lam_init), g_ref[...], lam_init)


def _attn_prompt(lam_vecs, subln_g, qb, kb, vb, *, batch, seq, lam_init):
    tq = ATTN_Q_ROWS
    nq = seq // tq
    T = batch * seq
    qspec = pl.BlockSpec((tq, HEAD_W), lambda b, h, i: (b * nq + i, h))
    kvspec = pl.BlockSpec((seq, HEAD_W), lambda b, h, i: (b, h))
    return pl.pallas_call(
        functools.partial(_attn_prompt_kernel, tq=tq, tk=ATTN_K_ROWS, lam_init=lam_init),
        grid=(batch, ATT_HEADS, nq),
        in_specs=[_const_spec((4, ATT_HD)), _const_spec((1, HEAD_W)), qspec, kvspec, kvspec],
        out_specs=qspec,
        out_shape=jax.ShapeDtypeStruct((T, ATT_WIDTH), BF16),
        compiler_params=_cparams(("parallel", "parallel", "arbitrary")),
        name="attn_prompt",
    )(lam_vecs, subln_g, qb, kb, vb)


def _attn_sample_kernel(lam_ref, g_ref, q_ref, kp_ref, vp_ref, kn_ref, vn_ref, o_ref, *, tq, lam_init):
    qq = _stack_maps(q_ref[...])
    s_past = _qk(qq, kp_ref[...].astype(BF16))
    s_new = _qk(qq, kn_ref[...])
    m = jnp.maximum(jnp.max(s_past, axis=1, keepdims=True), jnp.max(s_new, axis=1, keepdims=True))
    p_past = jnp.exp(s_past - m)
    p_new = jnp.exp(s_new - m)
    l = jnp.sum(p_past, axis=1, keepdims=True) + jnp.sum(p_new, axis=1, keepdims=True)
    acc = _dot(p_past.astype(BF16), vp_ref[...].astype(BF16)) + _dot(p_new.astype(BF16), vn_ref[...])
    o_ref[...] = _attn_finish(acc, l, tq, _diff_lambda(lam_ref, lam_init), g_ref[...], lam_init)


def _attn_sample(lam_vecs, subln_g, qb, k_past, v_past, kb, vb, *, batch, seq, past, lam_init):
    new = pl.BlockSpec((seq, HEAD_W), lambda b, h: (b, h))
    old = pl.BlockSpec((past, HEAD_W), lambda b, h: (b, h))
    return pl.pallas_call(
        functools.partial(_attn_sample_kernel, tq=seq, lam_init=lam_init),
        grid=(batch, ATT_HEADS),
        in_specs=[_const_spec((4, ATT_HD)), _const_spec((1, HEAD_W)), new, old, old, new, new],
        out_specs=new,
        out_shape=jax.ShapeDtypeStruct((batch * seq, ATT_WIDTH), BF16),
        compiler_params=_cparams(("parallel", "parallel")),
        name="attn_sample",
    )(lam_vecs, subln_g, qb, k_past, v_past, kb, vb)


def _ssm_kernel(u_ref, h0_ref, a_ref, b_ref, cr_ref, ci_ref,
                d_ref, wg_ref, bg_ref, s_ref, hout_ref, bu_ref, h_ref, tm_ref, *, steps, streams):
    n_state = a_ref.shape[1]
    n_blocks = b_ref.shape[0]
    in_w = b_ref.shape[1]
    st_w = n_state // n_blocks

    @pl.when(pl.program_id(0) == 0)
    def _():
        h_ref[...] = h0_ref[...]

    n_chunks = tm_ref.shape[0]
    chunk = lambda c: slice(c * LANES, (c + 1) * LANES)
    for b in range(streams):
        for c in range(n_chunks):
            tm_ref[c, pl.ds(b, steps, stride=streams), :] = u_ref[b, :, chunk(c)]
    u = jnp.concatenate([tm_ref[c] for c in range(n_chunks)], axis=1)
    u_b = u.astype(BF16)
    for gb in range(n_blocks):
        cols = slice(gb * in_w, (gb + 1) * in_w)
        bu = _dot(u_b[:, cols], b_ref[gb])
        bu_ref[:, gb * st_w:(gb + 1) * st_w] = bu[:, :st_w]
        bu_ref[:, n_state + gb * st_w:n_state + (gb + 1) * st_w] = bu[:, st_w:]

    for c in range(n_state // SCAN_COLS):
        re = slice(c * SCAN_COLS, (c + 1) * SCAN_COLS)
        im = slice(n_state + c * SCAN_COLS, n_state + (c + 1) * SCAN_COLS)
        ar = jnp.broadcast_to(a_ref[0:1, re], (streams, SCAN_COLS))
        ai = jnp.broadcast_to(a_ref[1:2, re], (streams, SCAN_COLS))

        def step(t, carry):
            hr, hi = carry
            rows = pl.ds(pl.multiple_of(t * streams, streams), streams)
            nr = ar * hr - ai * hi + bu_ref[rows, re]
            ni = ar * hi + ai * hr + bu_ref[rows, im]
            bu_ref[rows, re] = nr
            bu_ref[rows, im] = ni
            return nr, ni

        hr, hi = lax.fori_loop(0, steps, step, (h_ref[0, :, re], h_ref[1, :, re]), unroll=4)
        h_ref[0, :, re] = hr
        h_ref[1, :, re] = hi
    hout_ref[...] = h_ref[...]

    ys = []
    for gb in range(n_blocks):
        hr = bu_ref[:, gb * st_w:(gb + 1) * st_w].astype(BF16)
        hi = bu_ref[:, n_state + gb * st_w:n_state + (gb + 1) * st_w].astype(BF16)
        ys.append(_dot(hr, cr_ref[gb]) + _dot(hi, ci_ref[gb]))
    y = jax.nn.gelu(jnp.concatenate(ys, axis=1) + d_ref[...] * u)
    z = _dot(y.astype(BF16), wg_ref[...]) + bg_ref[...]
    s = y * jax.nn.sigmoid(z)
    for c in range(n_chunks):
        tm_ref[c] = s[:, chunk(c)]
    for b in range(streams):
        s_ref[b] = jnp.concatenate([tm_ref[c, pl.ds(b, steps, stride=streams), :] for c in range(n_chunks)],
                                   axis=1).astype(BF16)


def _ssm(u, h0, a, b, cr, ci, d, wg, bg, *, steps):
    streams, length, width = u.shape
    rows = steps * streams
    n_state = a.shape[1]
    blk = pl.BlockSpec((streams, steps, width), lambda i: (0, i, 0))
    consts = [h0, a, b, cr, ci, d, wg, bg]
    return pl.pallas_call(
        functools.partial(_ssm_kernel, steps=steps, streams=streams),
        grid=(length // steps,),
        in_specs=[blk] + [_const_spec(c.shape) for c in consts],
        out_specs=[blk, _const_spec(h0.shape)],
        out_shape=[jax.ShapeDtypeStruct(u.shape, BF16), jax.ShapeDtypeStruct(h0.shape, F32)],
        scratch_shapes=[pltpu.VMEM((rows, 2 * n_state), F32), pltpu.VMEM(h0.shape, F32),
                        pltpu.VMEM((width // LANES, rows, LANES), F32)],
        compiler_params=_cparams(("arbitrary",)),
        name="ssm",
    )(u, *consts)


def _ssm_params(a_re, a_im, log_dt, b_re, b_im, c_re, c_im):
    G, N, P = b_re.shape
    gb_n = SSM_GROUP_BLOCK
    nb = G // gb_n
    dt = jnp.exp(log_dt.astype(F32))[:, None]
    ar, ai = a_re.astype(F32), a_im.astype(F32)
    mag = jnp.exp(dt * ar)
    abar_r = mag * jnp.cos(dt * ai)
    abar_i = mag * jnp.sin(dt * ai)
    den = ar * ar + ai * ai
    nr, ni = abar_r - 1.0, abar_i
    coef_r = (nr * ar + ni * ai) / den
    coef_i = (ni * ar - nr * ai) / den
    br, bi = b_re.astype(F32), b_im.astype(F32)
    bbar_r = coef_r[..., None] * br - coef_i[..., None] * bi
    bbar_i = coef_r[..., None] * bi + coef_i[..., None] * br
    eye = jnp.eye(gb_n, dtype=F32)

    def in_block(bb):
        w = bb.reshape(nb, gb_n, N, P).transpose(0, 1, 3, 2)
        return jnp.einsum('bgpn,gh->bgphn', w, eye).reshape(nb, gb_n * P, gb_n * N)

    def out_block(cc):
        w = cc.reshape(nb, gb_n, P, N).transpose(0, 1, 3, 2)
        return jnp.einsum('bgnp,gh->bgnhp', w, eye).reshape(nb, gb_n * N, gb_n * P)

    b_bd = jnp.concatenate([in_block(bbar_r), in_block(bbar_i)], axis=2)
    a = jnp.stack([abar_r.reshape(-1), abar_i.reshape(-1)])
    return a, b_bd.astype(BF16), out_block(c_re.astype(F32)).astype(BF16), (-out_block(c_im.astype(F32))).astype(BF16)


def _outproj_kernel(x_ref, o_ref, s_ref, wo_ref, g2_ref, wq_ref, keys_ref, x1_ref, xn_ref, sc_ref):
    aw = o_ref.shape[1]
    x1 = x_ref[...] + _dot(o_ref[...], wo_ref[:aw, :]) + _dot(s_ref[...], wo_ref[aw:, :])
    x1_ref[...] = x1
    ms = jnp.mean(x1 * x1, axis=-1, keepdims=True)
    xn = x1 * lax.rsqrt(ms + EPS) * g2_ref[...]
    xn_ref[...] = xn
    q = _dot(xn.astype(BF16), wq_ref[...])
    for j in range(keys_ref.shape[0]):
        qj = q[:, j * PEER_HALF:(j + 1) * PEER_HALF].astype(BF16)
        sc_ref[j] = _qk(keys_ref[j], qj)


def _outproj(x2d, o_b, s_b, wo_bf, g2, wq_bf, keys_bf, *, tm):
    T, D = x2d.shape
    nset = keys_bf.shape[0]
    row = lambda w: pl.BlockSpec((tm, w), lambda i: (i, 0))
    return pl.pallas_call(
        _outproj_kernel,
        grid=(T // tm,),
        in_specs=[row(D), row(o_b.shape[1]), row(s_b.shape[1]), _const_spec(wo_bf.shape), _const_spec((1, D)),
                  _const_spec(wq_bf.shape), _const_spec(keys_bf.shape)],
        out_specs=[row(D), row(D), pl.BlockSpec((nset, PEER_KEYS, tm), lambda i: (0, 0, i))],
        out_shape=[jax.ShapeDtypeStruct((T, D), F32), jax.ShapeDtypeStruct((T, D), F32),
                   jax.ShapeDtypeStruct((nset, PEER_KEYS, T), F32)],
        compiler_params=_cparams(("parallel",)),
        name="outproj",
    )(x2d, o_b, s_b, wo_bf, g2, wq_bf, keys_bf)


def _extract_topk(vals, pos, payload, k):
    n, tt = vals.shape
    slot = lax.broadcasted_iota(jnp.int32, (k, tt), 0)
    out_v = jnp.zeros((k, tt), F32)
    out_p = jnp.zeros((k, tt), F32)
    for r in range(k):
        m = jnp.max(vals, axis=0, keepdims=True)
        first = jnp.min(jnp.where(vals == m, pos, jnp.inf), axis=0, keepdims=True)
        hit = pos == first
        picked = first if payload is None else jnp.max(jnp.where(hit, payload, -1.0), axis=0, keepdims=True)
        out_v = jnp.where(slot == r, m, out_v)
        out_p = jnp.where(slot == r, picked, out_p)
        vals = jnp.where(hit, -jnp.inf, vals)
    return out_v, out_p


def _staircase_candidates(a, ia, b, ib, k):
    rows = SUBLANES
    row = lax.broadcasted_iota(jnp.int32, (rows, a.shape[1]), 0)
    rowf = row.astype(F32)
    vals, pos, eid = [], [], []

    def add(v, p, e, valid=None):
        vals.append(v if valid is None else jnp.where(valid, v, -jnp.inf))
        pos.append(p)
        eid.append(e)

    i = 0
    while k // (i + 1) >= rows:
        for j0 in range(0, k // (i + 1), rows):
            add(a[i:i + 1] + b[j0:j0 + rows], float(i * k + j0) + rowf,
                ia[i:i + 1] * float(PEER_KEYS) + ib[j0:j0 + rows])
        i += 1
    while k // (i + 1) > 1:
        add(a[i:i + 1] + b[:rows], float(i * k) + rowf, ia[i:i + 1] * float(PEER_KEYS) + ib[:rows],
            valid=row < k // (i + 1))
        i += 1
    for i0 in range(i, k, rows):
        add(a[i0:i0 + rows] + b[0:1], (float(i0) + rowf) * float(k), ia[i0:i0 + rows] * float(PEER_KEYS) + ib[0:1])
    return tuple(jnp.concatenate(x, axis=0) for x in (vals, pos, eid))


def _topk_kernel(sc_ref, idx_ref, gate_ref):
    k = PEER_TOPK
    tt = sc_ref.shape[2]
    key_id = lax.broadcasted_iota(jnp.int32, (PEER_KEYS, tt), 0).astype(F32)

    def head(h, _):
        v0, i0 = _extract_topk(sc_ref[2 * h], key_id, None, k)
        v1, i1 = _extract_topk(sc_ref[2 * h + 1], key_id, None, k)
        fv, fe = _extract_topk(*_staircase_candidates(v0, i0, v1, i1, k), k)
        e = jnp.exp(fv - fv[0:1])
        rows = pl.ds(pl.multiple_of(h * k, k), k)
        gate_ref[0, rows, :] = e / jnp.sum(e, axis=0, keepdims=True)
        idx_ref[rows, :] = fe.astype(jnp.int32)
        return 0

    lax.fori_loop(0, PEER_HEADS, head, 0, unroll=4)


def _topk(scores):
    nset, nkeys, T = scores.shape
    tt = TOPK_TOKENS
    out = pl.BlockSpec((PEER_HEADS * PEER_TOPK, tt), lambda i: (0, i))
    return pl.pallas_call(
        _topk_kernel,
        grid=(T // tt,),
        in_specs=[pl.BlockSpec((nset, nkeys, tt), lambda i: (0, 0, i))],
        out_specs=[out, pl.BlockSpec((1, PEER_HEADS * PEER_TOPK, tt), lambda i: (i, 0, 0))],
        out_shape=[jax.ShapeDtypeStruct((PEER_HEADS * PEER_TOPK, T), jnp.int32),
                   jax.ShapeDtypeStruct((T // tt, PEER_HEADS * PEER_TOPK, tt), F32)],
        compiler_params=_cparams(("parallel",)),
        name="peer_topk",
    )(scores)


def _fold_pairs(vs, shift, keep):
    return [jnp.where(keep, a, b) + pltpu.roll(jnp.where(keep, b, a), shift, 0) for a, b in zip(vs[0::2], vs[1::2])]


def _expert_kernel(idx_ref, xn_ref, x1_ref, gate_ref, uv_hbm, y_ref, buf, sem, cb_ref, *, tokens, slots):
    n_sel = buf.shape[1]
    half = SUBLANES
    lane = lax.broadcasted_iota(jnp.int32, (n_sel, LANES), 1)
    sub = lax.broadcasted_iota(jnp.int32, (SUBLANES, LANES), 0)
    keep1, keep2, keep4 = sub % 2 == 0, sub % 4 < 2, sub < 4

    def issue(t, ks):
        slot = t % slots
        for k in ks:
            pltpu.make_async_copy(uv_hbm.at[idx_ref[t, k]], buf.at[slot, k], sem.at[slot]).start(priority=k % 2)

    def wait(t):
        slot = t % slots
        pltpu.make_async_copy(uv_hbm.at[pl.ds(0, n_sel)], buf.at[slot], sem.at[slot]).wait()

    n_groups = n_sel // SUBLANES
    per_chunk = n_sel // (2 * n_groups)

    def compute(t, t_next):
        slot = t % slots
        row = pl.ds(t, 1)
        chunk = lambda r: slice(r * LANES, (r + 1) * LANES)
        xrow = xn_ref[row, :]
        xa = jnp.concatenate([xrow[:, chunk(r)] for r in range(half)], axis=0)
        xb = jnp.concatenate([xrow[:, chunk(half + r)] for r in range(half)], axis=0)
        folded = []
        for g in range(n_groups):
            ps = []
            for k in range(g * SUBLANES, (g + 1) * SUBLANES):
                if t_next is not None and k % 16 == 0:
                    issue(t_next, range(k // 2, k // 2 + 8))
                w = buf[slot, k, 0].astype(F32)
                ps.append(w[:half] * xa + w[half:] * xb)
            ps = _fold_pairs(_fold_pairs(_fold_pairs(ps, 1, keep1), 2, keep2), 4, keep4)
            folded.append(ps[0])
        act = jnp.sum(jnp.concatenate(folded, axis=0), axis=1, keepdims=True)
        gate = jnp.sum(jnp.where(lane == t % LANES, gate_ref[t // LANES], 0.0), axis=1, keepdims=True)
        cb_ref[...] = jnp.broadcast_to(gate * jax.nn.gelu(act), cb_ref.shape)
        n_acc = 4
        acc_a = [jnp.zeros((half, LANES), F32)] * n_acc
        acc_b = [jnp.zeros((half, LANES), F32)] * n_acc
        for g in range(n_groups):
            for k in range(g * SUBLANES, (g + 1) * SUBLANES):
                if t_next is not None and k % 16 == 0:
                    issue(t_next, range((n_sel + k) // 2, (n_sel + k) // 2 + 8))
                c = jnp.broadcast_to(cb_ref[pl.ds(k, 1), :], (half, LANES))
                w = buf[slot, k, 1].astype(F32)
                acc_a[k % n_acc] = acc_a[k % n_acc] + c * w[:half]
                acc_b[k % n_acc] = acc_b[k % n_acc] + c * w[half:]
        out_a = (acc_a[0] + acc_a[1]) + (acc_a[2] + acc_a[3])
        out_b = (acc_b[0] + acc_b[1]) + (acc_b[2] + acc_b[3])
        out_row = jnp.concatenate([o[r:r + 1] for o in (out_a, out_b) for r in range(half)], axis=1)
        y_ref[row, :] = x1_ref[row, :] + out_row

    ahead = slots - 1

    def fill(t, c):
        issue(t, range(n_sel))
        return c

    def steady(t, c):
        wait(t)
        compute(t, t + ahead)
        return c

    def drain(t, c):
        wait(t)
        compute(t, None)
        return c

    lax.fori_loop(0, ahead, fill, 0)
    lax.fori_loop(0, tokens - ahead, steady, 0)
    lax.fori_loop(tokens - ahead, tokens, drain, 0)


def _experts(idx_tok, xn, x1, gates, uv):
    T, D = xn.shape
    n_sel = idx_tok.shape[1]
    tt = EXPERT_TOKENS
    row = pl.BlockSpec((tt, D), lambda i: (i, 0))
    return pl.pallas_call(
        functools.partial(_expert_kernel, tokens=tt, slots=EXPERT_SLOTS),
        grid=(T // tt,),
        in_specs=[pl.BlockSpec((tt, n_sel), lambda i: (i, 0), memory_space=pltpu.SMEM), row, row,
                  pl.BlockSpec((tt // LANES, n_sel, LANES), lambda i: (i, 0, 0)), pl.BlockSpec(memory_space=pl.ANY)],
        out_specs=row,
        out_shape=jax.ShapeDtypeStruct((T, D), F32),
        scratch_shapes=[pltpu.VMEM((EXPERT_SLOTS, n_sel) + uv.shape[1:], uv.dtype),
                        pltpu.SemaphoreType.DMA((EXPERT_SLOTS,)), pltpu.VMEM((n_sel, LANES), F32)],
        compiler_params=_cparams(("arbitrary",)),
        name="peer_experts",
    )(idx_tok, xn, x1, gates, uv)


def _pack_table_kernel(u_ref, v_ref, o_ref):
    rows = u_ref.shape[0]
    o_ref[:, 0] = u_ref[...].reshape(rows, -1, LANES).astype(BF16)
    o_ref[:, 1] = v_ref[...].reshape(rows, -1, LANES).astype(BF16)


def _pack_table(u, v):
    E, D = u.shape
    rows = PACK_ROWS
    row = pl.BlockSpec((rows, D), lambda i: (i, 0))
    return pl.pallas_call(
        _pack_table_kernel,
        grid=(E // rows,),
        in_specs=[row, row],
        out_specs=pl.BlockSpec((rows, 2, D // LANES, LANES), lambda i: (i, 0, 0, 0)),
        out_shape=jax.ShapeDtypeStruct((E, 2, D // LANES, LANES), BF16),
        compiler_params=_cparams(("parallel",)),
        name="pack_table",
    )(u, v)


def _rope_tables(pos):
    half = ROPE_DIM // 2
    inv_freq = ROPE_THETA ** (-jnp.arange(half, dtype=F32) * (2.0 / ROPE_DIM))
    ang = pos.astype(F32)[:, None] * inv_freq[None, :]
    cos, sin = jnp.cos(ang), jnp.sin(ang)
    n = pos.shape[0]
    pad = jnp.zeros((n, ATT_HD - ROPE_DIM), F32)
    zero = jnp.zeros((n, half), F32)
    c = jnp.concatenate([cos, cos, pad + 1.0], axis=1)
    sa = jnp.concatenate([-sin, zero, pad], axis=1)
    sb = jnp.concatenate([zero, sin, pad], axis=1)
    return tuple(jnp.tile(t, (1, LANES // ATT_HD)) for t in (c, sa, sb))


def _layer(x, pos, past_k, past_v, h0_re, h0_im, lam_init, p):
    B, L, D = x.shape
    T = B * L
    x2d = x.reshape(T, D)
    tm = min(INPROJ_ROWS, L)
    cos, sa, sb = _rope_tables(pos)
    qb, kf, kb, vf, vb, u = _inproj(x2d, p['g1'], p['w_in'], p['bd'], p['qg'], p['kg'], cos, sa, sb,
                                   tm=tm, pos_blocks=L // tm)
    if past_k is None:
        o = _attn_prompt(p['lam'], p['subln'], qb, kb, vb, batch=B, seq=L, lam_init=lam_init)
    else:
        past = past_k.shape[1]
        o = _attn_sample(p['lam'], p['subln'], qb, past_k.reshape(B * past, ATT_WIDTH),
                         past_v.reshape(B * past, ATT_WIDTH), kb, vb, batch=B, seq=L, past=past,
                         lam_init=lam_init)
    n_state = p['a'].shape[1]
    if h0_re is None:
        h0 = jnp.zeros((2, B, n_state), F32)
    else:
        h0 = jnp.stack([h0_re.reshape(B, n_state), h0_im.reshape(B, n_state)]).astype(F32)
    ssm_w = u.shape[1]
    s, h_fin = _ssm(u.reshape(B, L, ssm_w), h0, p['a'], p['b'], p['cr'], p['ci'], p['d'], p['wg'], p['bg'],
                    steps=min(SSM_STEPS, L))
    s = s.reshape(T, ssm_w)
    x1, xn, scores = _outproj(x2d, o, s, p['w_out'], p['g2'], p['wq'], p['keys'], tm=min(OUT_ROWS, T))
    idx_t, gates = _topk(scores)
    y = _experts(idx_t.T, xn, x1, gates, p['uv'])
    G = n_state // SSM_N
    return (y.reshape(B, L, D), kf.reshape(B, L, ATT_HEADS, 2, ATT_HD), vf.reshape(B, L, ATT_HEADS, HEAD_W),
            h_fin[0].reshape(B, G, SSM_N), h_fin[1].reshape(B, G, SSM_N))


def kernel(x_prompt, x_sample, cache_k, cache_v, state_ssm_re, state_ssm_im, norm1_g, w_in, q_norm_g, k_norm_g, lambda_q1, lambda_k1, lambda_q2, lambda_k2, subln_g, ssm_a_re, ssm_a_im, ssm_log_dt, ssm_b_re, ssm_b_im, ssm_c_re, ssm_c_im, ssm_d, w_glu, b_glu, w_out, norm2_g, w_peer_q, peer_keys, peer_u, peer_v):
    depth = w_in.shape[0]
    Lp, Ls, past = x_prompt.shape[1], x_sample.shape[1], cache_k.shape[2]
    pos_p = jnp.arange(Lp, dtype=jnp.int32)
    pos_s = past + jnp.arange(Ls, dtype=jnp.int32)
    lane = jnp.arange(LANES)
    bd = (lane[:, None] // ATT_HD == lane[None, :] // ATT_HD).astype(BF16)
    yp, ys = x_prompt, x_sample
    outs = [[] for _ in range(8)]
    for layer in range(depth):
        lam_init = 0.8 - 0.6 * math.exp(-0.3 * layer)
        a, b, cr, ci = _ssm_params(ssm_a_re[layer], ssm_a_im[layer], ssm_log_dt[layer], ssm_b_re[layer],
                                   ssm_b_im[layer], ssm_c_re[layer], ssm_c_im[layer])
        p = dict(
            g1=norm1_g[layer][None].astype(F32), w_in=w_in[layer].astype(BF16), bd=bd,
            qg=jnp.tile(q_norm_g[layer].astype(F32), LANES // ATT_HD)[None],
            kg=jnp.tile(k_norm_g[layer].astype(F32), LANES // ATT_HD)[None],
            lam=jnp.stack([lambda_q1[layer], lambda_k1[layer], lambda_q2[layer], lambda_k2[layer]]).astype(F32),
            subln=subln_g[layer][None].astype(F32),
            a=a, b=b, cr=cr, ci=ci, d=ssm_d[layer][None].astype(F32), wg=w_glu[layer].astype(BF16),
            bg=b_glu[layer][None].astype(F32), w_out=w_out[layer].astype(BF16),
            g2=norm2_g[layer][None].astype(F32), wq=w_peer_q[layer].astype(BF16),
            keys=peer_keys[layer].reshape(PEER_HEADS * 2, PEER_KEYS, PEER_HALF).astype(BF16),
            uv=_pack_table(peer_u[layer], peer_v[layer]))
        yp, kp, vp, hrp, hip = _layer(yp, pos_p, None, None, None, None, lam_init, p)
        ys, kk, vv, hrs, his = _layer(ys, pos_s, cache_k[layer], cache_v[layer], state_ssm_re[layer],
                                      state_ssm_im[layer], lam_init, p)
        for lst, val in zip(outs, (kp, vp, hrp, hip, kk, vv, hrs, his)):
            lst.append(val)
    return (yp, ys) + tuple(jnp.stack(l) for l in outs)
```

```python
import functools
import math

import jax
import jax.numpy as jnp
from jax import lax
from jax.experimental import pallas as pl
from jax.experimental.pallas import tpu as pltpu

F32 = jnp.float32
BF16 = jnp.bfloat16

CHUNK = 64
ATT_HEADS = 8
ATT_HD = 64
HEAD_W = 2 * ATT_HD
ATT_WIDTH = ATT_HEADS * HEAD_W
ROPE_DIM = ATT_HD // 4
ROPE_THETA = 500000.0
SSM_P = 16
SSM_N = 64
PEER_HEADS = 8
PEER_KEYS = 128
PEER_HALF = 128
PEER_TOPK = 16
EPS = 1e-6
NEG_INF = -1e30

LANES = 128
SUBLANES = 8
VMEM_LIMIT_BYTES = 56 * 1024 * 1024

INPROJ_ROWS = 256
ATTN_Q_ROWS = 512
ATTN_K_ROWS = 512
SSM_STEPS = 32
SSM_GROUP_BLOCK = 16
SCAN_COLS = 512
OUT_ROWS = 256
TOPK_TOKENS = 128
EXPERT_TOKENS = 512
EXPERT_SLOTS = 8
PACK_ROWS = 256


def _cparams(sem):
    return pltpu.CompilerParams(dimension_semantics=sem, vmem_limit_bytes=VMEM_LIMIT_BYTES)


def _const_spec(shape):
    nd = len(shape)
    return pl.BlockSpec(shape, lambda *_: (0,) * nd, pipeline_mode=pl.Buffered(1))


def _split_bf16(a):
    hi = a.astype(BF16)
    lo = (a - hi.astype(F32)).astype(BF16)
    return hi, lo


def _dot(a, b):
    return jnp.dot(a, b, preferred_element_type=F32)


def _inproj_kernel(x_ref, g1_ref, w_ref, bd_ref, qg_ref, kg_ref, cos_ref, sa_ref, sb_ref,
                   qb_ref, kt_ref, kb_ref, vf_ref, vb_ref, u_ref):
    x = x_ref[...]
    ms = jnp.mean(x * x, axis=-1, keepdims=True)
    xn = (x * lax.rsqrt(ms + EPS) * g1_ref[...]).astype(BF16)
    proj = _dot(xn, w_ref[...])
    bd = bd_ref[...]
    cos, sa, sb = cos_ref[...], sa_ref[...], sb_ref[...]

    def head_norm_rope(t, g):
        hi, lo = _split_bf16(t * t)
        ssq = _dot(hi, bd) + _dot(lo, bd)
        n = t * lax.rsqrt(ssq * (1.0 / ATT_HD) + EPS) * g
        return n * cos + pltpu.roll(n, LANES - ROPE_DIM // 2, 1) * sa + pltpu.roll(n, ROPE_DIM // 2, 1) * sb

    for h in range(ATT_HEADS):
        c0 = h * HEAD_W
        q = head_norm_rope(proj[:, c0:c0 + HEAD_W], qg_ref[...])
        qb_ref[:, c0:c0 + HEAD_W] = (q * (ATT_HD ** -0.5)).astype(BF16)
        k = head_norm_rope(proj[:, ATT_WIDTH + c0:ATT_WIDTH + c0 + HEAD_W], kg_ref[...])
        kt_ref[0, h] = k.T.reshape(2, ATT_HD, k.shape[0])
        kb_ref[:, c0:c0 + HEAD_W] = k.astype(BF16)
    v = proj[:, 2 * ATT_WIDTH:3 * ATT_WIDTH]
    vf_ref[...] = v
    vb_ref[...] = v.astype(BF16)
    u_ref[...] = proj[:, 3 * ATT_WIDTH:]


def _inproj(x2d, g1, w_bf, bd, qg, kg, cos, sa, sb, *, tm, pos_blocks):
    T, D = x2d.shape
    n_in = w_bf.shape[1]
    ssm_w = n_in - 3 * ATT_WIDTH
    row = lambda w: pl.BlockSpec((tm, w), lambda i: (i, 0))
    pos = pl.BlockSpec((tm, LANES), lambda i: (i % pos_blocks, 0))
    seq = tm * pos_blocks
    outs = [jax.ShapeDtypeStruct((T, ATT_WIDTH), BF16), jax.ShapeDtypeStruct((T // seq, ATT_HEADS, 2, ATT_HD, seq), F32),
            jax.ShapeDtypeStruct((T, ATT_WIDTH), BF16), jax.ShapeDtypeStruct((T, ATT_WIDTH), F32),
            jax.ShapeDtypeStruct((T, ATT_WIDTH), BF16), jax.ShapeDtypeStruct((T, ssm_w), F32)]
    return pl.pallas_call(
        _inproj_kernel,
        grid=(T // tm,),
        in_specs=[row(D), _const_spec((1, D)), _const_spec((D, n_in)), _const_spec((LANES, LANES)),
                  _const_spec((1, LANES)), _const_spec((1, LANES)), pos, pos, pos],
        out_specs=[row(ATT_WIDTH), pl.BlockSpec((1, ATT_HEADS, 2, ATT_HD, tm),
                                                lambda i: (i // pos_blocks, 0, 0, 0, i % pos_blocks))]
                  + [row(ATT_WIDTH)] * 3 + [row(ssm_w)],
        out_shape=outs,
        compiler_params=_cparams(("parallel",)),
        name="inproj",
    )(x2d, g1, w_bf, bd, qg, kg, cos, sa, sb)


def _diff_lambda(lam_ref, lam_init):
    lv = lam_ref[...]
    a = jnp.sum(lv[0:1] * lv[1:2], axis=-1, keepdims=True)
    b = jnp.sum(lv[2:3] * lv[3:4], axis=-1, keepdims=True)
    return jnp.exp(a) - jnp.exp(b) + lam_init


def _stack_maps(q):
    lane = lax.broadcasted_iota(jnp.int32, q.shape, 1)
    zero = jnp.zeros_like(q)
    return jnp.concatenate([jnp.where(lane < ATT_HD, q, zero), jnp.where(lane >= ATT_HD, q, zero)], axis=0)


def _attn_finish(acc, l, tq, lam, g, lam_init):
    o = acc[:tq] / l[:tq] - lam * (acc[tq:] / l[tq:])
    ms = jnp.mean(o * o, axis=-1, keepdims=True)
    return (o * lax.rsqrt(ms + EPS) * g * (1.0 - lam_init)).astype(BF16)


def _qk(qq, kb):
    return lax.dot_general(qq, kb, (((1,), (1,)), ((), ())), preferred_element_type=F32)


def _attn_prompt_kernel(lam_ref, g_ref, q_ref, k_ref, v_ref, o_ref, *, tq, tk, lam_init):
    i = pl.program_id(2)
    qq = _stack_maps(q_ref[...])

    def update(carry, s, vb):
        m, l, acc = carry
        m_new = jnp.maximum(m, jnp.max(s, axis=1, keepdims=True))
        alpha = jnp.exp(m - m_new)
        p = jnp.exp(s - m_new)
        l = alpha * l + jnp.sum(p, axis=1, keepdims=True)
        acc = alpha * acc + _dot(p.astype(BF16), vb)
        return m_new, l, acc

    def full_block(j, carry):
        rows = pl.ds(pl.multiple_of(j * tk, tk), tk)
        return update(carry, _qk(qq, k_ref[rows, :]), v_ref[rows, :])

    carry = (jnp.full((2 * tq, 1), NEG_INF, F32), jnp.zeros((2 * tq, 1), F32),
             jnp.zeros((2 * tq, HEAD_W), F32))
    q0 = i * tq
    k0 = (q0 // tk) * tk
    carry = lax.fori_loop(0, q0 // tk, full_block, carry)
    r = q0 + lax.broadcasted_iota(jnp.int32, (2 * tq, tk), 0) % tq
    c = k0 + lax.broadcasted_iota(jnp.int32, (2 * tq, tk), 1)
    for d in range(max(tq // tk, 1)):
        rows = pl.ds(pl.multiple_of(k0 + d * tk, tk), tk)
        s = jnp.where((c + d * tk) // CHUNK <= r // CHUNK, _qk(qq, k_ref[rows, :]), NEG_INF)
        carry = update(carry, s, v_ref[rows, :])
    _, l, acc = carry
    o_ref[...] = _attn_finish(acc, l, tq, _diff_lambda(lam_ref, lam_init), g_ref[...], lam_init)


def _attn_prompt(lam_vecs, subln_g, qb, kb, vb, *, batch, seq, lam_init):
    tq = ATTN_Q_ROWS
    nq = seq // tq
    T = batch * seq
    qspec = pl.BlockSpec((tq, HEAD_W), lambda b, h, i: (b * nq + i, h))
    kvspec = pl.BlockSpec((seq, HEAD_W), lambda b, h, i: (b, h))
    return pl.pallas_call(
        functools.partial(_attn_prompt_kernel, tq=tq, tk=ATTN_K_ROWS, lam_init=lam_init),
        grid=(batch, ATT_HEADS, nq),
        in_specs=[_const_spec((4, ATT_HD)), _const_spec((1, HEAD_W)), qspec, kvspec, kvspec],
        out_specs=qspec,
        out_shape=jax.ShapeDtypeStruct((T, ATT_WIDTH), BF16),
        compiler_params=_cparams(("parallel", "parallel", "arbitrary")),
        name="attn_prompt",
    )(lam_vecs, subln_g, qb, kb, vb)


def _attn_sample_kernel(lam_ref, g_ref, q_ref, kp_ref, vp_ref, kn_ref, vn_ref, o_ref, *, tq, lam_init):
    qq = _stack_maps(q_ref[...])
    kp = kp_ref[0, 0].reshape(HEAD_W, -1).astype(BF16)
    s_past = _dot(qq, kp)
    s_new = _qk(qq, kn_ref[...])
    m = jnp.maximum(jnp.max(s_past, axis=1, keepdims=True), jnp.max(s_new, axis=1, keepdims=True))
    p_past = jnp.exp(s_past - m)
    p_new = jnp.exp(s_new - m)
    l = jnp.sum(p_past, axis=1, keepdims=True) + jnp.sum(p_new, axis=1, keepdims=True)
    acc = _dot(p_past.astype(BF16), vp_ref[...].astype(BF16)) + _dot(p_new.astype(BF16), vn_ref[...])
    o_ref[...] = _attn_finish(acc, l, tq, _diff_lambda(lam_ref, lam_init), g_ref[...], lam_init)


def _attn_sample(lam_vecs, subln_g, qb, k_past, v_past, kb, vb, *, batch, seq, past, lam_init):
    new = pl.BlockSpec((seq, HEAD_W), lambda b, h: (b, h))
    old = pl.BlockSpec((past, HEAD_W), lambda b, h: (b, h))
    old_k = pl.BlockSpec((1, 1, 2, ATT_HD, past), lambda b, h: (b, h, 0, 0, 0))
    return pl.pallas_call(
        functools.partial(_attn_sample_kernel, tq=seq, lam_init=lam_init),
        grid=(batch, ATT_HEADS),
        in_specs=[_const_spec((4, ATT_HD)), _const_spec((1, HEAD_W)), new, old_k, old, new, new],
        out_specs=new,
        out_shape=jax.ShapeDtypeStruct((batch * seq, ATT_WIDTH), BF16),
        compiler_params=_cparams(("parallel", "parallel")),
        name="attn_sample",
    )(lam_vecs, subln_g, qb, k_past, v_past, kb, vb)


def _ssm_kernel(u_ref, h0_ref, a_ref, b_ref, cr_ref, ci_ref,
                d_ref, wg_ref, bg_ref, s_ref, hout_ref, bu_ref, h_ref, tm_ref, *, steps, streams):
    n_state = a_ref.shape[1]
    n_blocks = b_ref.shape[0]
    in_w = b_ref.shape[1]
    st_w = n_state // n_blocks

    @pl.when(pl.program_id(0) == 0)
    def _():
        h_ref[...] = h0_ref[...]

    n_chunks = tm_ref.shape[0]
    chunk = lambda c: slice(c * LANES, (c + 1) * LANES)
    for b in range(streams):
        for c in range(n_chunks):
            tm_ref[c, pl.ds(b, steps, stride=streams), :] = u_ref[b, :, chunk(c)]
    u = jnp.concatenate([tm_ref[c] for c in range(n_chunks)], axis=1)
    u_b = u.astype(BF16)
    for gb in range(n_blocks):
        cols = slice(gb * in_w, (gb + 1) * in_w)
        bu = _dot(u_b[:, cols], b_ref[gb])
        bu_ref[:, gb * st_w:(gb + 1) * st_w] = bu[:, :st_w]
        bu_ref[:, n_state + gb * st_w:n_state + (gb + 1) * st_w] = bu[:, st_w:]

    for c in range(n_state // SCAN_COLS):
        re = slice(c * SCAN_COLS, (c + 1) * SCAN_COLS)
        im = slice(n_state + c * SCAN_COLS, n_state + (c + 1) * SCAN_COLS)
        ar = jnp.broadcast_to(a_ref[0:1, re], (streams, SCAN_COLS))
        ai = jnp.broadcast_to(a_ref[1:2, re], (streams, SCAN_COLS))

        def step(t, carry):
            hr, hi = carry
            rows = pl.ds(pl.multiple_of(t * streams, streams), streams)
            nr = ar * hr - ai * hi + bu_ref[rows, re]
            ni = ar * hi + ai * hr + bu_ref[rows, im]
            bu_ref[rows, re] = nr
            bu_ref[rows, im] = ni
            return nr, ni

        hr, hi = lax.fori_loop(0, steps, step, (h_ref[0, :, re], h_ref[1, :, re]), unroll=4)
        h_ref[0, :, re] = hr
        h_ref[1, :, re] = hi
    hout_ref[...] = h_ref[...]

    ys = []
    for gb in range(n_blocks):
        hr = bu_ref[:, gb * st_w:(gb + 1) * st_w].astype(BF16)
        hi = bu_ref[:, n_state + gb * st_w:n_state + (gb + 1) * st_w].astype(BF16)
        ys.append(_dot(hr, cr_ref[gb]) + _dot(hi, ci_ref[gb]))
    y = jax.nn.gelu(jnp.concatenate(ys, axis=1) + d_ref[...] * u)
    z = _dot(y.astype(BF16), wg_ref[...]) + bg_ref[...]
    s = y * jax.nn.sigmoid(z)
    for c in range(n_chunks):
        tm_ref[c] = s[:, chunk(c)]
    for b in range(streams):
        s_ref[b] = jnp.concatenate([tm_ref[c, pl.ds(b, steps, stride=streams), :] for c in range(n_chunks)],
                                   axis=1).astype(BF16)


def _ssm(u, h0, a, b, cr, ci, d, wg, bg, *, steps):
    streams, length, width = u.shape
    rows = steps * streams
    n_state = a.shape[1]
    blk = pl.BlockSpec((streams, steps, width), lambda i: (0, i, 0))
    consts = [h0, a, b, cr, ci, d, wg, bg]
    return pl.pallas_call(
        functools.partial(_ssm_kernel, steps=steps, streams=streams),
        grid=(length // steps,),
        in_specs=[blk] + [_const_spec(c.shape) for c in consts],
        out_specs=[blk, _const_spec(h0.shape)],
        out_shape=[jax.ShapeDtypeStruct(u.shape, BF16), jax.ShapeDtypeStruct(h0.shape, F32)],
        scratch_shapes=[pltpu.VMEM((rows, 2 * n_state), F32), pltpu.VMEM(h0.shape, F32),
                        pltpu.VMEM((width // LANES, rows, LANES), F32)],
        compiler_params=_cparams(("arbitrary",)),
        name="ssm",
    )(u, *consts)


def _ssm_params(a_re, a_im, log_dt, b_re, b_im, c_re, c_im):
    G, N, P = b_re.shape
    gb_n = SSM_GROUP_BLOCK
    nb = G // gb_n
    dt = jnp.exp(log_dt.astype(F32))[:, None]
    ar, ai = a_re.astype(F32), a_im.astype(F32)
    mag = jnp.exp(dt * ar)
    abar_r = mag * jnp.cos(dt * ai)
    abar_i = mag * jnp.sin(dt * ai)
    den = ar * ar + ai * ai
    nr, ni = abar_r - 1.0, abar_i
    coef_r = (nr * ar + ni * ai) / den
    coef_i = (ni * ar - nr * ai) / den
    br, bi = b_re.astype(F32), b_im.astype(F32)
    bbar_r = coef_r[..., None] * br - coef_i[..., None] * bi
    bbar_i = coef_r[..., None] * bi + coef_i[..., None] * br
    eye = jnp.eye(gb_n, dtype=F32)

    def in_block(bb):
        w = bb.reshape(nb, gb_n, N, P).transpose(0, 1, 3, 2)
        return jnp.einsum('bgpn,gh->bgphn', w, eye).reshape(nb, gb_n * P, gb_n * N)

    def out_block(cc):
        w = cc.reshape(nb, gb_n, P, N).transpose(0, 1, 3, 2)
        return jnp.einsum('bgnp,gh->bgnhp', w, eye).reshape(nb, gb_n * N, gb_n * P)

    b_bd = jnp.concatenate([in_block(bbar_r), in_block(bbar_i)], axis=2)
    a = jnp.stack([abar_r.reshape(-1), abar_i.reshape(-1)])
    return a, b_bd.astype(BF16), out_block(c_re.astype(F32)).astype(BF16), (-out_block(c_im.astype(F32))).astype(BF16)


def _outproj_kernel(x_ref, o_ref, s_ref, wo_ref, g2_ref, wq_ref, keys_ref, x1_ref, xn_ref, sc_ref):
    aw = o_ref.shape[1]
    x1 = x_ref[...] + _dot(o_ref[...], wo_ref[:aw, :]) + _dot(s_ref[...], wo_ref[aw:, :])
    x1_ref[...] = x1
    ms = jnp.mean(x1 * x1, axis=-1, keepdims=True)
    xn = x1 * lax.rsqrt(ms + EPS) * g2_ref[...]
    xn_ref[...] = xn
    q = _dot(xn.astype(BF16), wq_ref[...])
    for j in range(keys_ref.shape[0]):
        qj = q[:, j * PEER_HALF:(j + 1) * PEER_HALF].astype(BF16)
        sc_ref[j] = _qk(keys_ref[j], qj)


def _outproj(x2d, o_b, s_b, wo_bf, g2, wq_bf, keys_bf, *, tm):
    T, D = x2d.shape
    nset = keys_bf.shape[0]
    row = lambda w: pl.BlockSpec((tm, w), lambda i: (i, 0))
    return pl.pallas_call(
        _outproj_kernel,
        grid=(T // tm,),
        in_specs=[row(D), row(o_b.shape[1]), row(s_b.shape[1]), _const_spec(wo_bf.shape), _const_spec((1, D)),
                  _const_spec(wq_bf.shape), _const_spec(keys_bf.shape)],
        out_specs=[row(D), row(D), pl.BlockSpec((nset, PEER_KEYS, tm), lambda i: (0, 0, i))],
        out_shape=[jax.ShapeDtypeStruct((T, D), F32), jax.ShapeDtypeStruct((T, D), F32),
                   jax.ShapeDtypeStruct((nset, PEER_KEYS, T), F32)],
        compiler_params=_cparams(("parallel",)),
        name="outproj",
    )(x2d, o_b, s_b, wo_bf, g2, wq_bf, keys_bf)


def _extract_topk(vals, pos, payload, k):
    n, tt = vals.shape
    slot = lax.broadcasted_iota(jnp.int32, (k, tt), 0)
    out_v = jnp.zeros((k, tt), F32)
    out_p = jnp.zeros((k, tt), F32)
    for r in range(k):
        m = jnp.max(vals, axis=0, keepdims=True)
        first = jnp.min(jnp.where(vals == m, pos, jnp.inf), axis=0, keepdims=True)
        hit = pos == first
        picked = first if payload is None else jnp.max(jnp.where(hit, payload, -1.0), axis=0, keepdims=True)
        out_v = jnp.where(slot == r, m, out_v)
        out_p = jnp.where(slot == r, picked, out_p)
        vals = jnp.where(hit, -jnp.inf, vals)
    return out_v, out_p


def _staircase_candidates(a, ia, b, ib, k):
    rows = SUBLANES
    row = lax.broadcasted_iota(jnp.int32, (rows, a.shape[1]), 0)
    rowf = row.astype(F32)
    vals, pos, eid = [], [], []

    def add(v, p, e, valid=None):
        vals.append(v if valid is None else jnp.where(valid, v, -jnp.inf))
        pos.append(p)
        eid.append(e)

    i = 0
    while k // (i + 1) >= rows:
        for j0 in range(0, k // (i + 1), rows):
            add(a[i:i + 1] + b[j0:j0 + rows], float(i * k + j0) + rowf,
                ia[i:i + 1] * float(PEER_KEYS) + ib[j0:j0 + rows])
        i += 1
    while k // (i + 1) > 1:
        add(a[i:i + 1] + b[:rows], float(i * k) + rowf, ia[i:i + 1] * float(PEER_KEYS) + ib[:rows],
            valid=row < k // (i + 1))
        i += 1
    for i0 in range(i, k, rows):
        add(a[i0:i0 + rows] + b[0:1], (float(i0) + rowf) * float(k), ia[i0:i0 + rows] * float(PEER_KEYS) + ib[0:1])
    return tuple(jnp.concatenate(x, axis=0) for x in (vals, pos, eid))


def _topk_kernel(sc_ref, idx_ref, gate_ref):
    k = PEER_TOPK
    tt = sc_ref.shape[2]
    key_id = lax.broadcasted_iota(jnp.int32, (PEER_KEYS, tt), 0).astype(F32)

    def head(h, _):
        v0, i0 = _extract_topk(sc_ref[2 * h], key_id, None, k)
        v1, i1 = _extract_topk(sc_ref[2 * h + 1], key_id, None, k)
        fv, fe = _extract_topk(*_staircase_candidates(v0, i0, v1, i1, k), k)
        e = jnp.exp(fv - fv[0:1])
        rows = pl.ds(pl.multiple_of(h * k, k), k)
        gate_ref[0, rows, :] = e / jnp.sum(e, axis=0, keepdims=True)
        idx_ref[rows, :] = fe.astype(jnp.int32)
        return 0

    lax.fori_loop(0, PEER_HEADS, head, 0, unroll=4)


def _topk(scores):
    nset, nkeys, T = scores.shape
    tt = TOPK_TOKENS
    out = pl.BlockSpec((PEER_HEADS * PEER_TOPK, tt), lambda i: (0, i))
    return pl.pallas_call(
        _topk_kernel,
        grid=(T // tt,),
        in_specs=[pl.BlockSpec((nset, nkeys, tt), lambda i: (0, 0, i))],
        out_specs=[out, pl.BlockSpec((1, PEER_HEADS * PEER_TOPK, tt), lambda i: (i, 0, 0))],
        out_shape=[jax.ShapeDtypeStruct((PEER_HEADS * PEER_TOPK, T), jnp.int32),
                   jax.ShapeDtypeStruct((T // tt, PEER_HEADS * PEER_TOPK, tt), F32)],
        compiler_params=_cparams(("parallel",)),
        name="peer_topk",
    )(scores)


def _fold_pairs(vs, shift, keep):
    return [jnp.where(keep, a, b) + pltpu.roll(jnp.where(keep, b, a), shift, 0) for a, b in zip(vs[0::2], vs[1::2])]


def _expert_kernel(idx_ref, xn_ref, x1_ref, gate_ref, uv_hbm, y_ref, buf, sem, cb_ref, *, tokens, slots):
    n_sel = buf.shape[1]
    half = SUBLANES
    lane = lax.broadcasted_iota(jnp.int32, (n_sel, LANES), 1)
    sub = lax.broadcasted_iota(jnp.int32, (SUBLANES, LANES), 0)
    keep1, keep2, keep4 = sub % 2 == 0, sub % 4 < 2, sub < 4

    def issue(t, ks):
        slot = t % slots
        for k in ks:
            pltpu.make_async_copy(uv_hbm.at[idx_ref[t, k]], buf.at[slot, k], sem.at[slot]).start(priority=k % 2)

    def wait(t):
        slot = t % slots
        pltpu.make_async_copy(uv_hbm.at[pl.ds(0, n_sel)], buf.at[slot], sem.at[slot]).wait()

    n_groups = n_sel // SUBLANES
    per_chunk = n_sel // (2 * n_groups)

    def compute(t, t_next):
        slot = t % slots
        row = pl.ds(t, 1)
        chunk = lambda r: slice(r * LANES, (r + 1) * LANES)
        xrow = xn_ref[row, :]
        xa = jnp.concatenate([xrow[:, chunk(r)] for r in range(half)], axis=0)
        xb = jnp.concatenate([xrow[:, chunk(half + r)] for r in range(half)], axis=0)
        folded = []
        for g in range(n_groups):
            ps = []
            for k in range(g * SUBLANES, (g + 1) * SUBLANES):
                if t_next is not None and k % 16 == 0:
                    issue(t_next, range(k // 2, k // 2 + 8))
                w = buf[slot, k, 0].astype(F32)
                ps.append(w[:half] * xa + w[half:] * xb)
            ps = _fold_pairs(_fold_pairs(_fold_pairs(ps, 1, keep1), 2, keep2), 4, keep4)
            folded.append(ps[0])
        act = jnp.sum(jnp.concatenate(folded, axis=0), axis=1, keepdims=True)
        gate = jnp.sum(jnp.where(lane == t % LANES, gate_ref[t // LANES], 0.0), axis=1, keepdims=True)
        cb_ref[...] = jnp.broadcast_to(gate * jax.nn.gelu(act), cb_ref.shape)
        n_acc = 4
        acc_a = [jnp.zeros((half, LANES), F32)] * n_acc
        acc_b = [jnp.zeros((half, LANES), F32)] * n_acc
        for g in range(n_groups):
            for k in range(g * SUBLANES, (g + 1) * SUBLANES):
                if t_next is not None and k % 16 == 0:
                    issue(t_next, range((n_sel + k) // 2, (n_sel + k) // 2 + 8))
                c = jnp.broadcast_to(cb_ref[pl.ds(k, 1), :], (half, LANES))
                w = buf[slot, k, 1].astype(F32)
                acc_a[k % n_acc] = acc_a[k % n_acc] + c * w[:half]
                acc_b[k % n_acc] = acc_b[k % n_acc] + c * w[half:]
        out_a = (acc_a[0] + acc_a[1]) + (acc_a[2] + acc_a[3])
        out_b = (acc_b[0] + acc_b[1]) + (acc_b[2] + acc_b[3])
        out_row = jnp.concatenate([o[r:r + 1] for o in (out_a, out_b) for r in range(half)], axis=1)
        y_ref[row, :] = x1_ref[row, :] + out_row

    ahead = slots - 1

    def fill(t, c):
        issue(t, range(n_sel))
        return c

    def steady(t, c):
        wait(t)
        compute(t, t + ahead)
        return c

    def drain(t, c):
        wait(t)
        compute(t, None)
        return c

    lax.fori_loop(0, ahead, fill, 0)
    lax.fori_loop(0, tokens - ahead, steady, 0)
    lax.fori_loop(tokens - ahead, tokens, drain, 0)


def _experts(idx_tok, xn, x1, gates, uv):
    T, D = xn.shape
    n_sel = idx_tok.shape[1]
    tt = EXPERT_TOKENS
    row = pl.BlockSpec((tt, D), lambda i: (i, 0))
    return pl.pallas_call(
        functools.partial(_expert_kernel, tokens=tt, slots=EXPERT_SLOTS),
        grid=(T // tt,),
        in_specs=[pl.BlockSpec((tt, n_sel), lambda i: (i, 0), memory_space=pltpu.SMEM), row, row,
                  pl.BlockSpec((tt // LANES, n_sel, LANES), lambda i: (i, 0, 0)), pl.BlockSpec(memory_space=pl.ANY)],
        out_specs=row,
        out_shape=jax.ShapeDtypeStruct((T, D), F32),
        scratch_shapes=[pltpu.VMEM((EXPERT_SLOTS, n_sel) + uv.shape[1:], uv.dtype),
                        pltpu.SemaphoreType.DMA((EXPERT_SLOTS,)), pltpu.VMEM((n_sel, LANES), F32)],
        compiler_params=_cparams(("arbitrary",)),
        name="peer_experts",
    )(idx_tok, xn, x1, gates, uv)


def _pack_table_kernel(u_ref, v_ref, o_ref):
    rows = u_ref.shape[0]
    o_ref[:, 0] = u_ref[...].reshape(rows, -1, LANES).astype(BF16)
    o_ref[:, 1] = v_ref[...].reshape(rows, -1, LANES).astype(BF16)


def _pack_table(u, v):
    E, D = u.shape
    rows = PACK_ROWS
    row = pl.BlockSpec((rows, D), lambda i: (i, 0))
    return pl.pallas_call(
        _pack_table_kernel,
        grid=(E // rows,),
        in_specs=[row, row],
        out_specs=pl.BlockSpec((rows, 2, D // LANES, LANES), lambda i: (i, 0, 0, 0)),
        out_shape=jax.ShapeDtypeStruct((E, 2, D // LANES, LANES), BF16),
        compiler_params=_cparams(("parallel",)),
        name="pack_table",
    )(u, v)


def _rope_tables(pos):
    half = ROPE_DIM // 2
    inv_freq = ROPE_THETA ** (-jnp.arange(half, dtype=F32) * (2.0 / ROPE_DIM))
    ang = pos.astype(F32)[:, None] * inv_freq[None, :]
    cos, sin = jnp.cos(ang), jnp.sin(ang)
    n = pos.shape[0]
    pad = jnp.zeros((n, ATT_HD - ROPE_DIM), F32)
    zero = jnp.zeros((n, half), F32)
    c = jnp.concatenate([cos, cos, pad + 1.0], axis=1)
    sa = jnp.concatenate([-sin, zero, pad], axis=1)
    sb = jnp.concatenate([zero, sin, pad], axis=1)
    return tuple(jnp.tile(t, (1, LANES // ATT_HD)) for t in (c, sa, sb))


def _layer(x, pos, past_k, past_v, h0_re, h0_im, lam_init, p):
    B, L, D = x.shape
    T = B * L
    x2d = x.reshape(T, D)
    tm = min(INPROJ_ROWS, L)
    cos, sa, sb = _rope_tables(pos)
    qb, kt, kb, vf, vb, u = _inproj(x2d, p['g1'], p['w_in'], p['bd'], p['qg'], p['kg'], cos, sa, sb,
                                   tm=tm, pos_blocks=L // tm)
    if past_k is None:
        o = _attn_prompt(p['lam'], p['subln'], qb, kb, vb, batch=B, seq=L, lam_init=lam_init)
    else:
        past = past_k.shape[1]
        o = _attn_sample(p['lam'], p['subln'], qb, past_k.transpose(0, 2, 3, 4, 1),
                         past_v.reshape(B * past, ATT_WIDTH), kb, vb, batch=B, seq=L, past=past,
                         lam_init=lam_init)
    n_state = p['a'].shape[1]
    if h0_re is None:
        h0 = jnp.zeros((2, B, n_state), F32)
    else:
        h0 = jnp.stack([h0_re.reshape(B, n_state), h0_im.reshape(B, n_state)]).astype(F32)
    ssm_w = u.shape[1]
    s, h_fin = _ssm(u.reshape(B, L, ssm_w), h0, p['a'], p['b'], p['cr'], p['ci'], p['d'], p['wg'], p['bg'],
                    steps=min(SSM_STEPS, L))
    s = s.reshape(T, ssm_w)
    x1, xn, scores = _outproj(x2d, o, s, p['w_out'], p['g2'], p['wq'], p['keys'], tm=min(OUT_ROWS, T))
    idx_t, gates = _topk(scores)
    y = _experts(idx_t.T, xn, x1, gates, p['uv'])
    G = n_state // SSM_N
    return (y.reshape(B, L, D), kt.transpose(0, 4, 1, 2, 3), vf.reshape(B, L, ATT_HEADS, HEAD_W),
            h_fin[0].reshape(B, G, SSM_N), h_fin[1].reshape(B, G, SSM_N))


def kernel(x_prompt, x_sample, cache_k, cache_v, state_ssm_re, state_ssm_im, norm1_g, w_in, q_norm_g, k_norm_g, lambda_q1, lambda_k1, lambda_q2, lambda_k2, subln_g, ssm_a_re, ssm_a_im, ssm_log_dt, ssm_b_re, ssm_b_im, ssm_c_re, ssm_c_im, ssm_d, w_glu, b_glu, w_out, norm2_g, w_peer_q, peer_keys, peer_u, peer_v):
    depth = w_in.shape[0]
    Lp, Ls, past = x_prompt.shape[1], x_sample.shape[1], cache_k.shape[2]
    pos_p = jnp.arange(Lp, dtype=jnp.int32)
    pos_s = past + jnp.arange(Ls, dtype=jnp.int32)
    lane = jnp.arange(LANES)
    bd = (lane[:, None] // ATT_HD == lane[None, :] // ATT_HD).astype(BF16)
    yp, ys = x_prompt, x_sample
    outs = [[] for _ in range(8)]
    for layer in range(depth):
        lam_init = 0.8 - 0.6 * math.exp(-0.3 * layer)
        a, b, cr, ci = _ssm_params(ssm_a_re[layer], ssm_a_im[layer], ssm_log_dt[layer], ssm_b_re[layer],
                                   ssm_b_im[layer], ssm_c_re[layer], ssm_c_im[layer])
        p = dict(
            g1=norm1_g[layer][None].astype(F32), w_in=w_in[layer].astype(BF16), bd=bd,
            qg=jnp.tile(q_norm_g[layer].astype(F32), LANES // ATT_HD)[None],
            kg=jnp.tile(k_norm_g[layer].astype(F32), LANES // ATT_HD)[None],
            lam=jnp.stack([lambda_q1[layer], lambda_k1[layer], lambda_q2[layer], lambda_k2[layer]]).astype(F32),
            subln=subln_g[layer][None].astype(F32),
            a=a, b=b, cr=cr, ci=ci, d=ssm_d[layer][None].astype(F32), wg=w_glu[layer].astype(BF16),
            bg=b_glu[layer][None].astype(F32), w_out=w_out[layer].astype(BF16),
            g2=norm2_g[layer][None].astype(F32), wq=w_peer_q[layer].astype(BF16),
            keys=peer_keys[layer].reshape(PEER_HEADS * 2, PEER_KEYS, PEER_HALF).astype(BF16),
            uv=_pack_table(peer_u[layer], peer_v[layer]))
        yp, kp, vp, hrp, hip = _layer(yp, pos_p, None, None, None, None, lam_init, p)
        ys, kk, vv, hrs, his = _layer(ys, pos_s, cache_k[layer], cache_v[layer], state_ssm_re[layer],
                                      state_ssm_im[layer], lam_init, p)
        for lst, val in zip(outs, (kp, vp, hrp, hip, kk, vv, hrs, his)):
            lst.append(val)
    return (yp, ys) + tuple(jnp.stack(l) for l in outs)
```

```python
import functools
import math

import jax
import jax.numpy as jnp
from jax import lax
from jax.experimental import pallas as pl
from jax.experimental.pallas import tpu as pltpu

F32 = jnp.float32
BF16 = jnp.bfloat16

CHUNK = 64
ATT_HEADS = 8
ATT_HD = 64
HEAD_W = 2 * ATT_HD
ATT_WIDTH = ATT_HEADS * HEAD_W
ROPE_DIM = ATT_HD // 4
ROPE_THETA = 500000.0
SSM_P = 16
SSM_N = 64
PEER_HEADS = 8
PEER_KEYS = 128
PEER_HALF = 128
PEER_TOPK = 16
EPS = 1e-6
NEG_INF = -1e30

LANES = 128
SUBLANES = 8
VMEM_LIMIT_BYTES = 56 * 1024 * 1024

INPROJ_ROWS = 256
ATTN_Q_ROWS = 512
ATTN_K_ROWS = 512
SSM_STEPS = 32
SSM_GROUP_BLOCK = 16
SCAN_COLS = 512
OUT_ROWS = 256
TOPK_TOKENS = 128
EXPERT_TOKENS = 512
EXPERT_SLOTS = 16
PACK_ROWS = 256


def _cparams(sem):
    return pltpu.CompilerParams(dimension_semantics=sem, vmem_limit_bytes=VMEM_LIMIT_BYTES)


def _const_spec(shape):
    nd = len(shape)
    return pl.BlockSpec(shape, lambda *_: (0,) * nd, pipeline_mode=pl.Buffered(1))


def _split_bf16(a):
    hi = a.astype(BF16)
    lo = (a - hi.astype(F32)).astype(BF16)
    return hi, lo


def _dot(a, b):
    return jnp.dot(a, b, preferred_element_type=F32)


def _inproj_kernel(x_ref, g1_ref, w_ref, bd_ref, qg_ref, kg_ref, cos_ref, sa_ref, sb_ref,
                   qb_ref, kt_ref, kb_ref, vf_ref, vb_ref, u_ref):
    x = x_ref[...]
    ms = jnp.mean(x * x, axis=-1, keepdims=True)
    xn = (x * lax.rsqrt(ms + EPS) * g1_ref[...]).astype(BF16)
    proj = _dot(xn, w_ref[...])
    bd = bd_ref[...]
    cos, sa, sb = cos_ref[...], sa_ref[...], sb_ref[...]

    def head_norm_rope(t, g):
        hi, lo = _split_bf16(t * t)
        ssq = _dot(hi, bd) + _dot(lo, bd)
        n = t * lax.rsqrt(ssq * (1.0 / ATT_HD) + EPS) * g
        return n * cos + pltpu.roll(n, LANES - ROPE_DIM // 2, 1) * sa + pltpu.roll(n, ROPE_DIM // 2, 1) * sb

    for h in range(ATT_HEADS):
        c0 = h * HEAD_W
        q = head_norm_rope(proj[:, c0:c0 + HEAD_W], qg_ref[...])
        qb_ref[:, c0:c0 + HEAD_W] = (q * (ATT_HD ** -0.5)).astype(BF16)
        k = head_norm_rope(proj[:, ATT_WIDTH + c0:ATT_WIDTH + c0 + HEAD_W], kg_ref[...])
        kt_ref[0, h] = k.T.reshape(2, ATT_HD, k.shape[0])
        kb_ref[:, c0:c0 + HEAD_W] = k.astype(BF16)
    v = proj[:, 2 * ATT_WIDTH:3 * ATT_WIDTH]
    vf_ref[...] = v
    vb_ref[...] = v.astype(BF16)
    u_ref[...] = proj[:, 3 * ATT_WIDTH:]


def _inproj(x2d, g1, w_bf, bd, qg, kg, cos, sa, sb, *, tm, pos_blocks):
    T, D = x2d.shape
    n_in = w_bf.shape[1]
    ssm_w = n_in - 3 * ATT_WIDTH
    row = lambda w: pl.BlockSpec((tm, w), lambda i: (i, 0))
    pos = pl.BlockSpec((tm, LANES), lambda i: (i % pos_blocks, 0))
    seq = tm * pos_blocks
    outs = [jax.ShapeDtypeStruct((T, ATT_WIDTH), BF16), jax.ShapeDtypeStruct((T // seq, ATT_HEADS, 2, ATT_HD, seq), F32),
            jax.ShapeDtypeStruct((T, ATT_WIDTH), BF16), jax.ShapeDtypeStruct((T, ATT_WIDTH), F32),
            jax.ShapeDtypeStruct((T, ATT_WIDTH), BF16), jax.ShapeDtypeStruct((T, ssm_w), F32)]
    return pl.pallas_call(
        _inproj_kernel,
        grid=(T // tm,),
        in_specs=[row(D), _const_spec((1, D)), _const_spec((D, n_in)), _const_spec((LANES, LANES)),
                  _const_spec((1, LANES)), _const_spec((1, LANES)), pos, pos, pos],
        out_specs=[row(ATT_WIDTH), pl.BlockSpec((1, ATT_HEADS, 2, ATT_HD, tm),
                                                lambda i: (i // pos_blocks, 0, 0, 0, i % pos_blocks))]
                  + [row(ATT_WIDTH)] * 3 + [row(ssm_w)],
        out_shape=outs,
        compiler_params=_cparams(("parallel",)),
        name="inproj",
    )(x2d, g1, w_bf, bd, qg, kg, cos, sa, sb)


def _diff_lambda(lam_ref, lam_init):
    lv = lam_ref[...]
    a = jnp.sum(lv[0:1] * lv[1:2], axis=-1, keepdims=True)
    b = jnp.sum(lv[2:3] * lv[3:4], axis=-1, keepdims=True)
    return jnp.exp(a) - jnp.exp(b) + lam_init


def _stack_maps(q):
    lane = lax.broadcasted_iota(jnp.int32, q.shape, 1)
    zero = jnp.zeros_like(q)
    return jnp.concatenate([jnp.where(lane < ATT_HD, q, zero), jnp.where(lane >= ATT_HD, q, zero)], axis=0)


def _attn_finish(acc, l, tq, lam, g, lam_init):
    o = acc[:tq] / l[:tq] - lam * (acc[tq:] / l[tq:])
    ms = jnp.mean(o * o, axis=-1, keepdims=True)
    return (o * lax.rsqrt(ms + EPS) * g * (1.0 - lam_init)).astype(BF16)


def _qk(qq, kb):
    return lax.dot_general(qq, kb, (((1,), (1,)), ((), ())), preferred_element_type=F32)


def _attn_prompt_kernel(lam_ref, g_ref, q_ref, k_ref, v_ref, o_ref, *, tq, tk, lam_init):
    i = pl.program_id(2)
    qq = _stack_maps(q_ref[...])

    def update(carry, s, vb):
        m, l, acc = carry
        m_new = jnp.maximum(m, jnp.max(s, axis=1, keepdims=True))
        alpha = jnp.exp(m - m_new)
        p = jnp.exp(s - m_new)
        l = alpha * l + jnp.sum(p, axis=1, keepdims=True)
        acc = alpha * acc + _dot(p.astype(BF16), vb)
        return m_new, l, acc

    def full_block(j, carry):
        rows = pl.ds(pl.multiple_of(j * tk, tk), tk)
        return update(carry, _qk(qq, k_ref[rows, :]), v_ref[rows, :])

    carry = (jnp.full((2 * tq, 1), NEG_INF, F32), jnp.zeros((2 * tq, 1), F32),
             jnp.zeros((2 * tq, HEAD_W), F32))
    q0 = i * tq
    k0 = (q0 // tk) * tk
    carry = lax.fori_loop(0, q0 // tk, full_block, carry)
    r = q0 + lax.broadcasted_iota(jnp.int32, (2 * tq, tk), 0) % tq
    c = k0 + lax.broadcasted_iota(jnp.int32, (2 * tq, tk), 1)
    for d in range(max(tq // tk, 1)):
        rows = pl.ds(pl.multiple_of(k0 + d * tk, tk), tk)
        s = jnp.where((c + d * tk) // CHUNK <= r // CHUNK, _qk(qq, k_ref[rows, :]), NEG_INF)
        carry = update(carry, s, v_ref[rows, :])
    _, l, acc = carry
    o_ref[...] = _attn_finish(acc, l, tq, _diff_lambda(lam_ref, lam_init), g_ref[...], lam_init)


def _attn_prompt(lam_vecs, subln_g, qb, kb, vb, *, batch, seq, lam_init):
    tq = ATTN_Q_ROWS
    nq = seq // tq
    T = batch * seq
    qspec = pl.BlockSpec((tq, HEAD_W), lambda b, h, i: (b * nq + i, h))
    kvspec = pl.BlockSpec((seq, HEAD_W), lambda b, h, i: (b, h))
    return pl.pallas_call(
        functools.partial(_attn_prompt_kernel, tq=tq, tk=ATTN_K_ROWS, lam_init=lam_init),
        grid=(batch, ATT_HEADS, nq),
        in_specs=[_const_spec((4, ATT_HD)), _const_spec((1, HEAD_W)), qspec, kvspec, kvspec],
        out_specs=qspec,
        out_shape=jax.ShapeDtypeStruct((T, ATT_WIDTH), BF16),
        compiler_params=_cparams(("parallel", "parallel", "arbitrary")),
        name="attn_prompt",
    )(lam_vecs, subln_g, qb, kb, vb)


def _attn_sample_kernel(lam_ref, g_ref, q_ref, kp_ref, vp_ref, kn_ref, vn_ref, o_ref, *, tq, lam_init):
    qq = _stack_maps(q_ref[...])
    kp = kp_ref[0, 0].reshape(HEAD_W, -1).astype(BF16)
    s_past = _dot(qq, kp)
    s_new = _qk(qq, kn_ref[...])
    m = jnp.maximum(jnp.max(s_past, axis=1, keepdims=True), jnp.max(s_new, axis=1, keepdims=True))
    p_past = jnp.exp(s_past - m)
    p_new = jnp.exp(s_new - m)
    l = jnp.sum(p_past, axis=1, keepdims=True) + jnp.sum(p_new, axis=1, keepdims=True)
    acc = _dot(p_past.astype(BF16), vp_ref[...].astype(BF16)) + _dot(p_new.astype(BF16), vn_ref[...])
    o_ref[...] = _attn_finish(acc, l, tq, _diff_lambda(lam_ref, lam_init), g_ref[...], lam_init)


def _attn_sample(lam_vecs, subln_g, qb, k_past, v_past, kb, vb, *, batch, seq, past, lam_init):
    new = pl.BlockSpec((seq, HEAD_W), lambda b, h: (b, h))
    old = pl.BlockSpec((past, HEAD_W), lambda b, h: (b, h))
    old_k = pl.BlockSpec((1, 1, 2, ATT_HD, past), lambda b, h: (b, h, 0, 0, 0))
    return pl.pallas_call(
        functools.partial(_attn_sample_kernel, tq=seq, lam_init=lam_init),
        grid=(batch, ATT_HEADS),
        in_specs=[_const_spec((4, ATT_HD)), _const_spec((1, HEAD_W)), new, old_k, old, new, new],
        out_specs=new,
        out_shape=jax.ShapeDtypeStruct((batch * seq, ATT_WIDTH), BF16),
        compiler_params=_cparams(("parallel", "parallel")),
        name="attn_sample",
    )(lam_vecs, subln_g, qb, k_past, v_past, kb, vb)


def _ssm_kernel(u_ref, h0_ref, a_ref, b_ref, cr_ref, ci_ref,
                d_ref, wg_ref, bg_ref, s_ref, hout_ref, bu_ref, h_ref, tm_ref, *, steps, streams):
    n_state = a_ref.shape[1]
    n_blocks = b_ref.shape[0]
    in_w = b_ref.shape[1]
    st_w = n_state // n_blocks

    @pl.when(pl.program_id(0) == 0)
    def _():
        h_ref[...] = h0_ref[...]

    n_chunks = tm_ref.shape[0]
    chunk = lambda c: slice(c * LANES, (c + 1) * LANES)
    for b in range(streams):
        for c in range(n_chunks):
            tm_ref[c, pl.ds(b, steps, stride=streams), :] = u_ref[b, :, chunk(c)]
    u = jnp.concatenate([tm_ref[c] for c in range(n_chunks)], axis=1)
    u_b = u.astype(BF16)
    for gb in range(n_blocks):
        cols = slice(gb * in_w, (gb + 1) * in_w)
        bu = _dot(u_b[:, cols], b_ref[gb])
        bu_ref[:, gb * st_w:(gb + 1) * st_w] = bu[:, :st_w]
        bu_ref[:, n_state + gb * st_w:n_state + (gb + 1) * st_w] = bu[:, st_w:]

    for c in range(n_state // SCAN_COLS):
        re = slice(c * SCAN_COLS, (c + 1) * SCAN_COLS)
        im = slice(n_state + c * SCAN_COLS, n_state + (c + 1) * SCAN_COLS)
        ar = jnp.broadcast_to(a_ref[0:1, re], (streams, SCAN_COLS))
        ai = jnp.broadcast_to(a_ref[1:2, re], (streams, SCAN_COLS))

        def step(t, carry):
            hr, hi = carry
            rows = pl.ds(pl.multiple_of(t * streams, streams), streams)
            nr = ar * hr - ai * hi + bu_ref[rows, re]
            ni = ar * hi + ai * hr + bu_ref[rows, im]
            bu_ref[rows, re] = nr
            bu_ref[rows, im] = ni
            return nr, ni

        hr, hi = lax.fori_loop(0, steps, step, (h_ref[0, :, re], h_ref[1, :, re]), unroll=4)
        h_ref[0, :, re] = hr
        h_ref[1, :, re] = hi
    hout_ref[...] = h_ref[...]

    ys = []
    for gb in range(n_blocks):
        hr = bu_ref[:, gb * st_w:(gb + 1) * st_w].astype(BF16)
        hi = bu_ref[:, n_state + gb * st_w:n_state + (gb + 1) * st_w].astype(BF16)
        ys.append(_dot(hr, cr_ref[gb]) + _dot(hi, ci_ref[gb]))
    y = jax.nn.gelu(jnp.concatenate(ys, axis=1) + d_ref[...] * u)
    z = _dot(y.astype(BF16), wg_ref[...]) + bg_ref[...]
    s = y * jax.nn.sigmoid(z)
    for c in range(n_chunks):
        tm_ref[c] = s[:, chunk(c)]
    for b in range(streams):
        s_ref[b] = jnp.concatenate([tm_ref[c, pl.ds(b, steps, stride=streams), :] for c in range(n_chunks)],
                                   axis=1).astype(BF16)


def _ssm(u, h0, a, b, cr, ci, d, wg, bg, *, steps):
    streams, length, width = u.shape
    rows = steps * streams
    n_state = a.shape[1]
    blk = pl.BlockSpec((streams, steps, width), lambda i: (0, i, 0))
    consts = [h0, a, b, cr, ci, d, wg, bg]
    return pl.pallas_call(
        functools.partial(_ssm_kernel, steps=steps, streams=streams),
        grid=(length // steps,),
        in_specs=[blk] + [_const_spec(c.shape) for c in consts],
        out_specs=[blk, _const_spec(h0.shape)],
        out_shape=[jax.ShapeDtypeStruct(u.shape, BF16), jax.ShapeDtypeStruct(h0.shape, F32)],
        scratch_shapes=[pltpu.VMEM((rows, 2 * n_state), F32), pltpu.VMEM(h0.shape, F32),
                        pltpu.VMEM((width // LANES, rows, LANES), F32)],
        compiler_params=_cparams(("arbitrary",)),
        name="ssm",
    )(u, *consts)


def _ssm_params(a_re, a_im, log_dt, b_re, b_im, c_re, c_im):
    G, N, P = b_re.shape
    gb_n = SSM_GROUP_BLOCK
    nb = G // gb_n
    dt = jnp.exp(log_dt.astype(F32))[:, None]
    ar, ai = a_re.astype(F32), a_im.astype(F32)
    mag = jnp.exp(dt * ar)
    abar_r = mag * jnp.cos(dt * ai)
    abar_i = mag * jnp.sin(dt * ai)
    den = ar * ar + ai * ai
    nr, ni = abar_r - 1.0, abar_i
    coef_r = (nr * ar + ni * ai) / den
    coef_i = (ni * ar - nr * ai) / den
    br, bi = b_re.astype(F32), b_im.astype(F32)
    bbar_r = coef_r[..., None] * br - coef_i[..., None] * bi
    bbar_i = coef_r[..., None] * bi + coef_i[..., None] * br
    eye = jnp.eye(gb_n, dtype=F32)

    def in_block(bb):
        w = bb.reshape(nb, gb_n, N, P).transpose(0, 1, 3, 2)
        return jnp.einsum('bgpn,gh->bgphn', w, eye).reshape(nb, gb_n * P, gb_n * N)

    def out_block(cc):
        w = cc.reshape(nb, gb_n, P, N).transpose(0, 1, 3, 2)
        return jnp.einsum('bgnp,gh->bgnhp', w, eye).reshape(nb, gb_n * N, gb_n * P)

    b_bd = jnp.concatenate([in_block(bbar_r), in_block(bbar_i)], axis=2)
    a = jnp.stack([abar_r.reshape(-1), abar_i.reshape(-1)])
    return a, b_bd.astype(BF16), out_block(c_re.astype(F32)).astype(BF16), (-out_block(c_im.astype(F32))).astype(BF16)


def _outproj_kernel(x_ref, o_ref, s_ref, wo_ref, g2_ref, wq_ref, keys_ref, x1_ref, xn_ref, sc_ref):
    aw = o_ref.shape[1]
    x1 = x_ref[...] + _dot(o_ref[...], wo_ref[:aw, :]) + _dot(s_ref[...], wo_ref[aw:, :])
    x1_ref[...] = x1
    ms = jnp.mean(x1 * x1, axis=-1, keepdims=True)
    xn = x1 * lax.rsqrt(ms + EPS) * g2_ref[...]
    xn_ref[...] = xn
    q = _dot(xn.astype(BF16), wq_ref[...])
    for j in range(keys_ref.shape[0]):
        qj = q[:, j * PEER_HALF:(j + 1) * PEER_HALF].astype(BF16)
        sc_ref[j] = _qk(keys_ref[j], qj)


def _outproj(x2d, o_b, s_b, wo_bf, g2, wq_bf, keys_bf, *, tm):
    T, D = x2d.shape
    nset = keys_bf.shape[0]
    row = lambda w: pl.BlockSpec((tm, w), lambda i: (i, 0))
    return pl.pallas_call(
        _outproj_kernel,
        grid=(T // tm,),
        in_specs=[row(D), row(o_b.shape[1]), row(s_b.shape[1]), _const_spec(wo_bf.shape), _const_spec((1, D)),
                  _const_spec(wq_bf.shape), _const_spec(keys_bf.shape)],
        out_specs=[row(D), row(D), pl.BlockSpec((nset, PEER_KEYS, tm), lambda i: (0, 0, i))],
        out_shape=[jax.ShapeDtypeStruct((T, D), F32), jax.ShapeDtypeStruct((T, D), F32),
                   jax.ShapeDtypeStruct((nset, PEER_KEYS, T), F32)],
        compiler_params=_cparams(("parallel",)),
        name="outproj",
    )(x2d, o_b, s_b, wo_bf, g2, wq_bf, keys_bf)


def _extract_topk(vals, pos, payload, k):
    n, tt = vals.shape
    slot = lax.broadcasted_iota(jnp.int32, (k, tt), 0)
    out_v = jnp.zeros((k, tt), F32)
    out_p = jnp.zeros((k, tt), F32)
    for r in range(k):
        m = jnp.max(vals, axis=0, keepdims=True)
        first = jnp.min(jnp.where(vals == m, pos, jnp.inf), axis=0, keepdims=True)
        hit = pos == first
        picked = first if payload is None else jnp.max(jnp.where(hit, payload, -1.0), axis=0, keepdims=True)
        out_v = jnp.where(slot == r, m, out_v)
        out_p = jnp.where(slot == r, picked, out_p)
        vals = jnp.where(hit, -jnp.inf, vals)
    return out_v, out_p


def _staircase_candidates(a, ia, b, ib, k):
    rows = SUBLANES
    row = lax.broadcasted_iota(jnp.int32, (rows, a.shape[1]), 0)
    rowf = row.astype(F32)
    vals, pos, eid = [], [], []

    def add(v, p, e, valid=None):
        vals.append(v if valid is None else jnp.where(valid, v, -jnp.inf))
        pos.append(p)
        eid.append(e)

    i = 0
    while k // (i + 1) >= rows:
        for j0 in range(0, k // (i + 1), rows):
            add(a[i:i + 1] + b[j0:j0 + rows], float(i * k + j0) + rowf,
                ia[i:i + 1] * float(PEER_KEYS) + ib[j0:j0 + rows])
        i += 1
    while k // (i + 1) > 1:
        add(a[i:i + 1] + b[:rows], float(i * k) + rowf, ia[i:i + 1] * float(PEER_KEYS) + ib[:rows],
            valid=row < k // (i + 1))
        i += 1
    for i0 in range(i, k, rows):
        add(a[i0:i0 + rows] + b[0:1], (float(i0) + rowf) * float(k), ia[i0:i0 + rows] * float(PEER_KEYS) + ib[0:1])
    return tuple(jnp.concatenate(x, axis=0) for x in (vals, pos, eid))


def _topk_kernel(sc_ref, idx_ref, gate_ref):
    k = PEER_TOPK
    tt = sc_ref.shape[2]
    key_id = lax.broadcasted_iota(jnp.int32, (PEER_KEYS, tt), 0).astype(F32)

    def head(h, _):
        v0, i0 = _extract_topk(sc_ref[2 * h], key_id, None, k)
        v1, i1 = _extract_topk(sc_ref[2 * h + 1], key_id, None, k)
        fv, fe = _extract_topk(*_staircase_candidates(v0, i0, v1, i1, k), k)
        e = jnp.exp(fv - fv[0:1])
        rows = pl.ds(pl.multiple_of(h * k, k), k)
        gate_ref[0, rows, :] = e / jnp.sum(e, axis=0, keepdims=True)
        idx_ref[rows, :] = fe.astype(jnp.int32)
        return 0

    lax.fori_loop(0, PEER_HEADS, head, 0, unroll=4)


def _topk(scores):
    nset, nkeys, T = scores.shape
    tt = TOPK_TOKENS
    out = pl.BlockSpec((PEER_HEADS * PEER_TOPK, tt), lambda i: (0, i))
    return pl.pallas_call(
        _topk_kernel,
        grid=(T // tt,),
        in_specs=[pl.BlockSpec((nset, nkeys, tt), lambda i: (0, 0, i))],
        out_specs=[out, pl.BlockSpec((1, PEER_HEADS * PEER_TOPK, tt), lambda i: (i, 0, 0))],
        out_shape=[jax.ShapeDtypeStruct((PEER_HEADS * PEER_TOPK, T), jnp.int32),
                   jax.ShapeDtypeStruct((T // tt, PEER_HEADS * PEER_TOPK, tt), F32)],
        compiler_params=_cparams(("parallel",)),
        name="peer_topk",
    )(scores)


def _fold_pairs(vs, shift, keep):
    return [jnp.where(keep, a, b) + pltpu.roll(jnp.where(keep, b, a), shift, 0) for a, b in zip(vs[0::2], vs[1::2])]


def _expert_kernel(idx_ref, xn_ref, x1_ref, gate_ref, uv_hbm, y_ref, buf, sem, cb_ref, *, tokens, slots):
    n_sel = buf.shape[1]
    half = SUBLANES
    lane = lax.broadcasted_iota(jnp.int32, (n_sel, LANES), 1)
    sub = lax.broadcasted_iota(jnp.int32, (SUBLANES, LANES), 0)
    keep1, keep2, keep4 = sub % 2 == 0, sub % 4 < 2, sub < 4

    def issue(t, ks):
        slot = t % slots
        for k in ks:
            pltpu.make_async_copy(uv_hbm.at[idx_ref[t, k]], buf.at[slot, k], sem.at[slot]).start(priority=k % 2)

    def wait(t):
        slot = t % slots
        pltpu.make_async_copy(uv_hbm.at[pl.ds(0, n_sel)], buf.at[slot], sem.at[slot]).wait()

    n_groups = n_sel // SUBLANES
    per_chunk = n_sel // (2 * n_groups)

    chunk = lambda r: slice(r * LANES, (r + 1) * LANES)

    def slab(t):
        xrow = xn_ref[pl.ds(t, 1), :]
        xa = jnp.zeros((half, LANES), F32)
        xb = jnp.zeros((half, LANES), F32)
        for r in range(half):
            xa = jnp.where(sub == r, jnp.broadcast_to(xrow[:, chunk(r)], (half, LANES)), xa)
            xb = jnp.where(sub == r, jnp.broadcast_to(xrow[:, chunk(half + r)], (half, LANES)), xb)
        return xa, xb

    def compute(t, t_next, xa, xb):
        slot = t % slots
        row = pl.ds(t, 1)
        folded = []
        for g in range(n_groups):
            ps = []
            for k in range(g * SUBLANES, (g + 1) * SUBLANES):
                w = buf[slot, k, 0].astype(F32)
                ps.append(w[:half] * xa + w[half:] * xb)
                if t_next is not None and k % 16 == 15:
                    issue(t_next, range(k // 16 * 8, k // 16 * 8 + 8))
            ps = _fold_pairs(_fold_pairs(_fold_pairs(ps, 1, keep1), 2, keep2), 4, keep4)
            folded.append(ps[0])
        act = jnp.sum(jnp.concatenate(folded, axis=0), axis=1, keepdims=True)
        gate = jnp.sum(jnp.where(lane == t % LANES, gate_ref[t // LANES], 0.0), axis=1, keepdims=True)
        cb_ref[...] = jnp.broadcast_to(gate * jax.nn.gelu(act), cb_ref.shape)
        n_acc = 4
        acc_a = [jnp.zeros((half, LANES), F32)] * n_acc
        acc_b = [jnp.zeros((half, LANES), F32)] * n_acc
        for g in range(n_groups):
            for k in range(g * SUBLANES, (g + 1) * SUBLANES):
                c = jnp.broadcast_to(cb_ref[pl.ds(k, 1), :], (half, LANES))
                w = buf[slot, k, 1].astype(F32)
                acc_a[k % n_acc] = acc_a[k % n_acc] + c * w[:half]
                acc_b[k % n_acc] = acc_b[k % n_acc] + c * w[half:]
                if t_next is not None and k % 16 == 15:
                    issue(t_next, range((n_sel + k) // 16 * 8, (n_sel + k) // 16 * 8 + 8))
        out_a = (acc_a[0] + acc_a[1]) + (acc_a[2] + acc_a[3])
        out_b = (acc_b[0] + acc_b[1]) + (acc_b[2] + acc_b[3])
        out_row = jnp.concatenate([o[r:r + 1] for o in (out_a, out_b) for r in range(half)], axis=1)
        y_ref[row, :] = x1_ref[row, :] + out_row

    ahead = slots - 1

    def fill(t, c):
        issue(t, range(n_sel))
        return c

    def steady(t, x):
        wait(t)
        compute(t, t + ahead, *x)
        return slab(t + 1)

    def drain(t, x):
        wait(t)
        compute(t, None, *x)
        return slab(jnp.minimum(t + 1, tokens - 1))

    lax.fori_loop(0, ahead, fill, 0)
    x = lax.fori_loop(0, tokens - ahead, steady, slab(0))
    lax.fori_loop(tokens - ahead, tokens, drain, x)


def _experts(idx_tok, xn, x1, gates, uv):
    T, D = xn.shape
    n_sel = idx_tok.shape[1]
    tt = EXPERT_TOKENS
    row = pl.BlockSpec((tt, D), lambda i: (i, 0))
    return pl.pallas_call(
        functools.partial(_expert_kernel, tokens=tt, slots=EXPERT_SLOTS),
        grid=(T // tt,),
        in_specs=[pl.BlockSpec((tt, n_sel), lambda i: (i, 0), memory_space=pltpu.SMEM), row, row,
                  pl.BlockSpec((tt // LANES, n_sel, LANES), lambda i: (i, 0, 0)), pl.BlockSpec(memory_space=pl.ANY)],
        out_specs=row,
        out_shape=jax.ShapeDtypeStruct((T, D), F32),
        scratch_shapes=[pltpu.VMEM((EXPERT_SLOTS, n_sel) + uv.shape[1:], uv.dtype),
                        pltpu.SemaphoreType.DMA((EXPERT_SLOTS,)), pltpu.VMEM((n_sel, LANES), F32)],
        compiler_params=_cparams(("arbitrary",)),
        name="peer_experts",
    )(idx_tok, xn, x1, gates, uv)


def _pack_table_kernel(u_ref, v_ref, o_ref):
    rows = u_ref.shape[0]
    o_ref[:, 0] = u_ref[...].reshape(rows, -1, LANES).astype(BF16)
    o_ref[:, 1] = v_ref[...].reshape(rows, -1, LANES).astype(BF16)


def _pack_table(u, v):
    E, D = u.shape
    rows = PACK_ROWS
    row = pl.BlockSpec((rows, D), lambda i: (i, 0))
    return pl.pallas_call(
        _pack_table_kernel,
        grid=(E // rows,),
        in_specs=[row, row],
        out_specs=pl.BlockSpec((rows, 2, D // LANES, LANES), lambda i: (i, 0, 0, 0)),
        out_shape=jax.ShapeDtypeStruct((E, 2, D // LANES, LANES), BF16),
        compiler_params=_cparams(("parallel",)),
        name="pack_table",
    )(u, v)


def _rope_tables(pos):
    half = ROPE_DIM // 2
    inv_freq = ROPE_THETA ** (-jnp.arange(half, dtype=F32) * (2.0 / ROPE_DIM))
    ang = pos.astype(F32)[:, None] * inv_freq[None, :]
    cos, sin = jnp.cos(ang), jnp.sin(ang)
    n = pos.shape[0]
    pad = jnp.zeros((n, ATT_HD - ROPE_DIM), F32)
    zero = jnp.zeros((n, half), F32)
    c = jnp.concatenate([cos, cos, pad + 1.0], axis=1)
    sa = jnp.concatenate([-sin, zero, pad], axis=1)
    sb = jnp.concatenate([zero, sin, pad], axis=1)
    return tuple(jnp.tile(t, (1, LANES // ATT_HD)) for t in (c, sa, sb))


def _layer(x, pos, past_k, past_v, h0_re, h0_im, lam_init, p):
    B, L, D = x.shape
    T = B * L
    x2d = x.reshape(T, D)
    tm = min(INPROJ_ROWS, L)
    cos, sa, sb = _rope_tables(pos)
    qb, kt, kb, vf, vb, u = _inproj(x2d, p['g1'], p['w_in'], p['bd'], p['qg'], p['kg'], cos, sa, sb,
                                   tm=tm, pos_blocks=L // tm)
    if past_k is None:
        o = _attn_prompt(p['lam'], p['subln'], qb, kb, vb, batch=B, seq=L, lam_init=lam_init)
    else:
        past = past_k.shape[1]
        o = _attn_sample(p['lam'], p['subln'], qb, past_k.transpose(0, 2, 3, 4, 1),
                         past_v.reshape(B * past, ATT_WIDTH), kb, vb, batch=B, seq=L, past=past,
                         lam_init=lam_init)
    n_state = p['a'].shape[1]
    if h0_re is None:
        h0 = jnp.zeros((2, B, n_state), F32)
    else:
        h0 = jnp.stack([h0_re.reshape(B, n_state), h0_im.reshape(B, n_state)]).astype(F32)
    ssm_w = u.shape[1]
    s, h_fin = _ssm(u.reshape(B, L, ssm_w), h0, p['a'], p['b'], p['cr'], p['ci'], p['d'], p['wg'], p['bg'],
                    steps=min(SSM_STEPS, L))
    s = s.reshape(T, ssm_w)
    x1, xn, scores = _outproj(x2d, o, s, p['w_out'], p['g2'], p['wq'], p['keys'], tm=min(OUT_ROWS, T))
    idx_t, gates = _topk(scores)
    y = _experts(idx_t.T, xn, x1, gates, p['uv'])
    G = n_state // SSM_N
    return (y.reshape(B, L, D), kt.transpose(0, 4, 1, 2, 3), vf.reshape(B, L, ATT_HEADS, HEAD_W),
            h_fin[0].reshape(B, G, SSM_N), h_fin[1].reshape(B, G, SSM_N))


def kernel(x_prompt, x_sample, cache_k, cache_v, state_ssm_re, state_ssm_im, norm1_g, w_in, q_norm_g, k_norm_g, lambda_q1, lambda_k1, lambda_q2, lambda_k2, subln_g, ssm_a_re, ssm_a_im, ssm_log_dt, ssm_b_re, ssm_b_im, ssm_c_re, ssm_c_im, ssm_d, w_glu, b_glu, w_out, norm2_g, w_peer_q, peer_keys, peer_u, peer_v):
    depth = w_in.shape[0]
    Lp, Ls, past = x_prompt.shape[1], x_sample.shape[1], cache_k.shape[2]
    pos_p = jnp.arange(Lp, dtype=jnp.int32)
    pos_s = past + jnp.arange(Ls, dtype=jnp.int32)
    lane = jnp.arange(LANES)
    bd = (lane[:, None] // ATT_HD == lane[None, :] // ATT_HD).astype(BF16)
    yp, ys = x_prompt, x_sample
    outs = [[] for _ in range(8)]
    for layer in range(depth):
        lam_init = 0.8 - 0.6 * math.exp(-0.3 * layer)
        a, b, cr, ci = _ssm_params(ssm_a_re[layer], ssm_a_im[layer], ssm_log_dt[layer], ssm_b_re[layer],
                                   ssm_b_im[layer], ssm_c_re[layer], ssm_c_im[layer])
        p = dict(
            g1=norm1_g[layer][None].astype(F32), w_in=w_in[layer].astype(BF16), bd=bd,
            qg=jnp.tile(q_norm_g[layer].astype(F32), LANES // ATT_HD)[None],
            kg=jnp.tile(k_norm_g[layer].astype(F32), LANES // ATT_HD)[None],
            lam=jnp.stack([lambda_q1[layer], lambda_k1[layer], lambda_q2[layer], lambda_k2[layer]]).astype(F32),
            subln=subln_g[layer][None].astype(F32),
            a=a, b=b, cr=cr, ci=ci, d=ssm_d[layer][None].astype(F32), wg=w_glu[layer].astype(BF16),
            bg=b_glu[layer][None].astype(F32), w_out=w_out[layer].astype(BF16),
            g2=norm2_g[layer][None].astype(F32), wq=w_peer_q[layer].astype(BF16),
            keys=peer_keys[layer].reshape(PEER_HEADS * 2, PEER_KEYS, PEER_HALF).astype(BF16),
            uv=_pack_table(peer_u[layer], peer_v[layer]))
        yp, kp, vp, hrp, hip = _layer(yp, pos_p, None, None, None, None, lam_init, p)
        ys, kk, vv, hrs, his = _layer(ys, pos_s, cache_k[layer], cache_v[layer], state_ssm_re[layer],
                                      state_ssm_im[layer], lam_init, p)
        for lst, val in zip(outs, (kp, vp, hrp, hip, kk, vv, hrs, his)):
            lst.append(val)
    return (yp, ys) + tuple(jnp.stack(l) for l in outs)
```
